```python
import jax, jax.numpy as jnp
from jax import lax
import numpy as np

D_MODEL = 1024
BATCH = 16
SEQ = 2048
DEPTH = 2
DEC_BATCH = 8
DEC_SEQ = 8192
PAST_LEN = 128

SG_HEADS = 4
SG_WIDTH = D_MODEL // 2
SG_HEAD_DIM = SG_WIDTH // SG_HEADS
SG_CHUNK = 128
GLA_HEADS = 4
GLA_K_WIDTH = D_MODEL // 4
GLA_V_WIDTH = D_MODEL // 2
GLA_DK = GLA_K_WIDTH // GLA_HEADS
GLA_DV = GLA_V_WIDTH // GLA_HEADS
GLA_CHUNK = 64
GLA_GATE_RANK = 16
GLA_GATE_NORMALIZER = 16.0
SPLIT_SIZES = (SG_WIDTH, SG_WIDTH, GLA_K_WIDTH, GLA_K_WIDTH, GLA_V_WIDTH, GLA_V_WIDTH, GLA_GATE_RANK, GLA_GATE_RANK)
SPLIT_IDX = tuple(int(i) for i in np.cumsum(SPLIT_SIZES)[:-1])
IN_COLS = int(sum(SPLIT_SIZES))
D_MIX = SG_WIDTH + GLA_V_WIDTH
D_FF = ((8 * D_MODEL // 3 + 127) // 128) * 128
CONV_WIDTH = 3
N_MOD = 6
EPS = 1e-6

kernel_name = "hymba_style_sg_gla_bidir_encoder"


def rmsnorm(x, g):
    xf = x.astype(jnp.float32)
    y = xf * lax.rsqrt(jnp.mean(xf * xf, axis=-1, keepdims=True) + EPS)
    return (y * g.astype(jnp.float32)).astype(x.dtype)


def layernorm(x, g):
    xf = x.astype(jnp.float32)
    xc = xf - jnp.mean(xf, axis=-1, keepdims=True)
    y = xc * lax.rsqrt(jnp.mean(xc * xc, axis=-1, keepdims=True) + EPS)
    return (y * g.astype(jnp.float32)).astype(x.dtype)


def spatial_gating(u, v, w_s, b_s, g_vn):
    B, T, _ = u.shape
    n = T // SG_CHUNK
    shp = (B, n, SG_CHUNK, SG_HEADS, SG_HEAD_DIM)
    vn = layernorm(v.reshape(shp), g_vn.reshape(SG_HEADS, SG_HEAD_DIM))
    mixed = jnp.einsum('hij,bnjhc->bnihc', w_s, vn) + b_s.T[:, :, None]
    return (u.reshape(shp) * mixed).reshape(B, T, SG_WIDTH)


def gla_chunked(q, k, v, g):
    B, T, H, _ = q.shape
    n = T // GLA_CHUNK
    rs = lambda a: a.reshape(B, n, GLA_CHUNK, H, a.shape[-1])
    q, k, v, g = rs(q), rs(k), rs(v), rs(g)
    b = jnp.cumsum(g, axis=2)
    b_last = b[:, :, -1:]
    q_dec = q * jnp.exp(b)
    k_dec = k * jnp.exp(-b)
    mask = jnp.tril(jnp.ones((GLA_CHUNK, GLA_CHUNK), dtype=bool))
    scores = jnp.where(mask, jnp.einsum('bnihd,bnjhd->bnhij', q_dec, k_dec), 0.0)
    o_intra = jnp.einsum('bnhij,bnjhv->bnihv', scores, v)
    d_state = jnp.einsum('bnjhd,bnjhv->nbhdv', k * jnp.exp(b_last - b), v)
    decay = jnp.exp(b_last[:, :, 0]).transpose(1, 0, 2, 3)
    q_seq = q_dec.transpose(1, 0, 2, 3, 4)

    def step(S, inp):
        qn, dn, dSn = inp
        o = jnp.einsum('bihd,bhdv->bihv', qn, S)
        return dn[..., None] * S + dSn, o

    S0 = jnp.zeros((B, H, GLA_DK, GLA_DV), jnp.float32)
    _, o_inter = lax.scan(step, S0, (q_seq, decay, d_state))
    o = o_intra + o_inter.transpose(1, 0, 2, 3, 4)
    return o.reshape(B, T, H, GLA_DV)


def mixer(h, w_in, w_s, b_s, g_vn, g_out_a, w_gf, b_gf, w_gb, b_gb, g_out_b, w_out):
    B, T, _ = h.shape
    f32 = jnp.float32
    proj = h @ w_in
    u_a, v_a, q, k, v_b, r, lf, lb = jnp.split(proj, SPLIT_IDX, axis=-1)
    a = spatial_gating(u_a, v_a, w_s, b_s, g_vn)
    a = rmsnorm(a.reshape(B, T, SG_HEADS, SG_HEAD_DIM), g_out_a.reshape(SG_HEADS, SG_HEAD_DIM)).reshape(B, T, SG_WIDTH)
    qh = q.reshape(B, T, GLA_HEADS, GLA_DK).astype(f32) * (GLA_DK ** -0.5)
    kh = k.reshape(B, T, GLA_HEADS, GLA_DK).astype(f32)
    vh = v_b.reshape(B, T, GLA_HEADS, GLA_DV).astype(f32)
    gf = (jax.nn.log_sigmoid((lf @ w_gf + b_gf).astype(f32)) / GLA_GATE_NORMALIZER).reshape(B, T, GLA_HEADS, GLA_DK)
    gb = (jax.nn.log_sigmoid((lb @ w_gb + b_gb).astype(f32)) / GLA_GATE_NORMALIZER).reshape(B, T, GLA_HEADS, GLA_DK)
    flip = lambda t: jnp.flip(t, axis=1)
    o_fwd = gla_chunked(qh, kh, vh, gf)
    o_bwd = flip(gla_chunked(flip(qh), flip(kh), flip(vh), flip(gb)))
    o = rmsnorm(o_fwd + o_bwd, g_out_b.reshape(GLA_HEADS, GLA_DV)).reshape(B, T, GLA_V_WIDTH)
    o = (o * jax.nn.silu(r.astype(f32))).astype(h.dtype)
    return jnp.concatenate([a.astype(h.dtype), o], axis=-1) @ w_out


def conv_ffn(h, w_up, w_conv, b_conv, w_down):
    up = h @ w_up
    up = lax.conv_general_dilated(
        up, w_conv[:, None, :].astype(up.dtype), window_strides=(1,),
        padding=((CONV_WIDTH // 2, CONV_WIDTH // 2),),
        dimension_numbers=('NWC', 'WIO', 'NWC'),
        feature_group_count=2 * D_FF) + b_conv
    gate, val = jnp.split(up, 2, axis=-1)
    return (jax.nn.silu(gate) * val) @ w_down


def encoder_trunk(x, c, w_ada, b_ada, g_pre_mix, g_post_mix, g_pre_ffn, g_post_ffn,
                  w_in, w_s, b_s, g_vn, g_out_a, w_gf, b_gf, w_gb, b_gb, g_out_b, w_out,
                  w_up, w_conv, b_conv, w_down):
    for l in range(DEPTH):
        mod = jax.nn.silu(c) @ w_ada[l] + b_ada[l]
        shift1, scale1, gate1, shift2, scale2, gate2 = [m[:, None, :] for m in jnp.split(mod, N_MOD, axis=-1)]
        h = rmsnorm(x, g_pre_mix[l]) * (1 + scale1) + shift1
        y = mixer(h, w_in[l], w_s[l], b_s[l], g_vn[l], g_out_a[l], w_gf[l], b_gf[l],
                  w_gb[l], b_gb[l], g_out_b[l], w_out[l])
        x = x + gate1 * rmsnorm(y, g_post_mix[l])
        h = rmsnorm(x, g_pre_ffn[l]) * (1 + scale2) + shift2
        y = conv_ffn(h, w_up[l], w_conv[l], b_conv[l], w_down[l])
        x = x + gate2 * rmsnorm(y, g_post_ffn[l])
    return x


def setup_inputs(seed: int = 0) -> dict:
    key = jax.random.key(seed)
    ks = jax.random.split(key, 28)
    L = DEPTH
    nrm = lambda k, shape, s: jax.random.normal(k, shape, jnp.float32) * s
    gain = lambda k, shape: 1.0 + 0.02 * jax.random.normal(k, shape, jnp.float32)
    return {
        "x_prompt": nrm(ks[0], (BATCH, SEQ, D_MODEL), 1.0),
        "x_sample": nrm(ks[1], (DEC_BATCH, DEC_SEQ, D_MODEL), 1.0),
        "c_prompt": nrm(ks[2], (BATCH, D_MODEL), 1.0),
        "c_sample": nrm(ks[3], (DEC_BATCH, D_MODEL), 1.0),
        "w_ada": nrm(ks[4], (L, D_MODEL, N_MOD * D_MODEL), 0.5 * D_MODEL ** -0.5),
        "b_ada": nrm(ks[5], (L, N_MOD * D_MODEL), 0.02),
        "g_pre_mix": gain(ks[6], (L, D_MODEL)),
        "g_post_mix": gain(ks[7], (L, D_MODEL)),
        "g_pre_ffn": gain(ks[8], (L, D_MODEL)),
        "g_post_ffn": gain(ks[9], (L, D_MODEL)),
        "w_in": nrm(ks[10], (L, D_MODEL, IN_COLS), D_MODEL ** -0.5),
        "w_s": nrm(ks[11], (L, SG_HEADS, SG_CHUNK, SG_CHUNK), SG_CHUNK ** -0.5),
        "b_s": gain(ks[12], (L, SG_HEADS, SG_CHUNK)),
        "g_vn": gain(ks[13], (L, SG_WIDTH)),
        "g_out_a": gain(ks[14], (L, SG_WIDTH)),
        "w_gf": nrm(ks[15], (L, GLA_GATE_RANK, GLA_K_WIDTH), GLA_GATE_RANK ** -0.5),
        "b_gf": nrm(ks[16], (L, GLA_K_WIDTH), 0.1),
        "w_gb": nrm(ks[17], (L, GLA_GATE_RANK, GLA_K_WIDTH), GLA_GATE_RANK ** -0.5),
        "b_gb": nrm(ks[18], (L, GLA_K_WIDTH), 0.1),
        "g_out_b": gain(ks[19], (L, GLA_V_WIDTH)),
        "w_out": nrm(ks[20], (L, D_MIX, D_MODEL), D_MIX ** -0.5),
        "w_up": nrm(ks[21], (L, D_MODEL, 2 * D_FF), D_MODEL ** -0.5),
        "w_conv": nrm(ks[22], (L, CONV_WIDTH, 2 * D_FF), CONV_WIDTH ** -0.5),
        "b_conv": nrm(ks[23], (L, 2 * D_FF), 0.02),
        "w_down": nrm(ks[24], (L, D_FF, D_MODEL), D_FF ** -0.5),
    }


def reference(x_prompt, x_sample, c_prompt, c_sample, w_ada, b_ada, g_pre_mix, g_post_mix,
              g_pre_ffn, g_post_ffn, w_in, w_s, b_s, g_vn, g_out_a, w_gf, b_gf, w_gb, b_gb,
              g_out_b, w_out, w_up, w_conv, b_conv, w_down):
    params = (w_ada, b_ada, g_pre_mix, g_post_mix, g_pre_ffn, g_post_ffn, w_in, w_s, b_s, g_vn,
              g_out_a, w_gf, b_gf, w_gb, b_gb, g_out_b, w_out, w_up, w_conv, b_conv, w_down)
    y_prompt = encoder_trunk(x_prompt, c_prompt, *params)
    y_sample = encoder_trunk(x_sample, c_sample, *params)
    return (y_prompt, y_sample)
```

```python
import functools

import jax
import jax.numpy as jnp
from jax import lax
from jax.experimental import pallas as pl
from jax.experimental.pallas import tpu as pltpu

F32 = jnp.float32
BF16 = jnp.bfloat16

D_MODEL = 1024
N_MOD = 6
EPS = 1e-6
SG_HEADS = 4
SG_WIDTH = 512
SG_HEAD_DIM = 128
SG_CHUNK = 128
GLA_HEADS = 4
GLA_DK = 64
GLA_DV = 128
GLA_K_WIDTH = 256
GLA_V_WIDTH = 512
GLA_GATE_RANK = 16
GLA_GATE_NORMALIZER = 16.0
GLA_CHUNK = 64
COL_U, COL_VA, COL_Q, COL_V, COL_R, COL_LR, COL_END = 0, 512, 1024, 1536, 2048, 2560, 2592
LR_PAD = 128
D_FF = 2816
FF_BLOCK = 256
CONV_HALO = 8

MIX_TILE = 512
FFN_TILE = 256
VMEM_LIMIT = 56 * 1024 * 1024


def _sigmoid(x):
    return 1.0 / (1.0 + jnp.exp(-x))


def _log_sigmoid(x):
    return jnp.minimum(x, 0.0) - jnp.log(1.0 + jnp.exp(-jnp.abs(x)))


def _rms_scale(x):
    return x * lax.rsqrt(jnp.mean(x * x, axis=-1, keepdims=True) + EPS)


def _const_spec(shape):
    nd = len(shape)
    return pl.BlockSpec(shape, lambda *_: (0,) * nd, pipeline_mode=pl.Buffered(1))


def _mod_kernel(c_ref, w_ref, b_ref, o_ref):
    c = c_ref[...]
    s = (c * _sigmoid(c)).astype(BF16)
    o_ref[0] = jnp.dot(s, w_ref[0].astype(BF16), preferred_element_type=F32) + b_ref[0]


def _modulation(c_all, w_ada, b_ada):
    depth = w_ada.shape[0]
    rows = c_all.shape[0]
    cols = w_ada.shape[2]
    bn = 1536
    return pl.pallas_call(
        _mod_kernel,
        grid=(depth, cols // bn),
        in_specs=[
            pl.BlockSpec((rows, D_MODEL), lambda l, j: (0, 0)),
            pl.BlockSpec((1, D_MODEL, bn), lambda l, j: (l, 0, j)),
            pl.BlockSpec((1, 1, bn), lambda l, j: (l, 0, j)),
        ],
        out_specs=pl.BlockSpec((1, rows, bn), lambda l, j: (l, 0, j)),
        out_shape=jax.ShapeDtypeStruct((depth, rows, cols), F32),
        compiler_params=pltpu.CompilerParams(vmem_limit_bytes=VMEM_LIMIT),
        name="adaln_mod",
    )(c_all, w_ada, b_ada.reshape(depth, 1, cols))


def _gla_chunk(q, k, v, gpre, s_ref, reverse):
    c = q.shape[0]
    g = _log_sigmoid(gpre) * (1.0 / GLA_GATE_NORMALIZER)
    g_hi = g.astype(BF16)
    g_lo = (g - g_hi.astype(F32)).astype(BF16)
    g_cat = jnp.concatenate([g_hi, g_lo], axis=1)
    row = lax.broadcasted_iota(jnp.int32, (c, c), 0)
    col = lax.broadcasted_iota(jnp.int32, (c, c), 1)
    causal = (col >= row) if reverse else (col <= row)
    tri = jnp.where(causal, 1.0, 0.0).astype(BF16)
    bb = jnp.dot(tri, g_cat, preferred_element_type=F32)
    b = bb[:, :GLA_K_WIDTH] + bb[:, GLA_K_WIDTH:]
    b_last = b[0:1, :] if reverse else b[c - 1:c, :]
    q_dec = (q * (GLA_DK ** -0.5) * jnp.exp(b)).astype(BF16)
    k_dec = (k * jnp.exp(-b)).astype(BF16)
    k_end = (k * jnp.exp(b_last - b)).astype(BF16)
    ones = jnp.ones((c, GLA_DV), BF16)
    tot = lax.dot_general(g_cat, ones, (((0,), (0,)), ((), ())), preferred_element_type=F32)
    decay_col = jnp.exp(tot[:GLA_K_WIDTH] + tot[GLA_K_WIDTH:])
    outs = []
    for h in range(GLA_HEADS):
        ks = slice(h * GLA_DK, (h + 1) * GLA_DK)
        vs = slice(h * GLA_DV, (h + 1) * GLA_DV)
        qd, kd, ke, vh = q_dec[:, ks], k_dec[:, ks], k_end[:, ks], v[:, vs]
        s_old = s_ref[ks, :]
        scores = lax.dot_general(qd, kd, (((1,), (1,)), ((), ())), preferred_element_type=F32)
        scores = jnp.where(causal, scores, 0.0).astype(BF16)
        o_h = jnp.dot(scores, vh, preferred_element_type=F32)
        o_h = o_h + jnp.dot(qd, s_old.astype(BF16), preferred_element_type=F32)
        d_s = lax.dot_general(ke, vh, (((0,), (0,)), ((), ())), preferred_element_type=F32)
        s_ref[ks, :] = decay_col[ks, :] * s_old + d_s
        outs.append(o_h)
    return jnp.concatenate(outs, axis=1)


def _mix_fwd_kernel(x_ref, mod_ref, gpre_ref, wmain_ref, wlr_ref, wg_ref, bg_ref, ws_ref, bs_ref, gvn_ref, goa_ref,
                    a_ref, qk_ref, v_ref, r_ref, lr_ref, of_ref,
                    s_ref, qkf_ref, gate_ref):
    tt = x_ref.shape[1]

    @pl.when(pl.program_id(1) == 0)
    def _():
        s_ref[...] = jnp.zeros_like(s_ref)

    x = x_ref[0]
    mod = mod_ref[0]
    h = _rms_scale(x) * gpre_ref[...] * (1.0 + mod[1:2]) + mod[0:1]
    hb = h.astype(BF16)
    proj = jnp.dot(hb, wmain_ref[...], preferred_element_type=F32)
    lr = jnp.dot(hb, wlr_ref[...], preferred_element_type=F32).astype(BF16)
    lr_ref[0] = lr
    gate_ref[...] = jnp.dot(lr, wg_ref[:, :GLA_K_WIDTH], preferred_element_type=F32) + bg_ref[:, :GLA_K_WIDTH]
    qkf_ref[...] = proj[:, COL_Q:COL_V]
    qk_ref[0] = proj[:, COL_Q:COL_V].astype(BF16)
    v_ref[0] = proj[:, COL_V:COL_R].astype(BF16)
    r_ref[0] = proj[:, COL_R:COL_LR].astype(BF16)

    n_sg = tt // SG_CHUNK
    a_heads = []
    for hd in range(SG_HEADS):
        cs = slice(hd * SG_HEAD_DIM, (hd + 1) * SG_HEAD_DIM)
        vh = proj[:, COL_VA + hd * SG_HEAD_DIM:COL_VA + (hd + 1) * SG_HEAD_DIM]
        xc = vh - jnp.mean(vh, axis=-1, keepdims=True)
        vn = (xc * lax.rsqrt(jnp.mean(xc * xc, axis=-1, keepdims=True) + EPS) * gvn_ref[:, cs]).astype(BF16)
        vn_cat = jnp.concatenate([vn[i * SG_CHUNK:(i + 1) * SG_CHUNK] for i in range(n_sg)], axis=1)
        mixed_cat = jnp.dot(ws_ref[hd], vn_cat, preferred_element_type=F32)
        mixed = jnp.concatenate([mixed_cat[:, i * SG_CHUNK:(i + 1) * SG_CHUNK] + bs_ref[hd] for i in range(n_sg)], axis=0)
        ah = proj[:, COL_U + hd * SG_HEAD_DIM:COL_U + (hd + 1) * SG_HEAD_DIM] * mixed
        a_heads.append((_rms_scale(ah) * goa_ref[:, cs]).astype(BF16))
    a_ref[0] = jnp.concatenate(a_heads, axis=1)

    def body(i, carry):
        r0 = pl.multiple_of(i * GLA_CHUNK, GLA_CHUNK)
        rows = pl.ds(r0, GLA_CHUNK)
        qk = qkf_ref[rows, :]
        o = _gla_chunk(qk[:, :GLA_K_WIDTH], qk[:, GLA_K_WIDTH:], v_ref[0, rows, :], gate_ref[rows, :], s_ref, False)
        of_ref[0, rows, :] = o.astype(of_ref.dtype)
        return carry

    lax.fori_loop(0, tt // GLA_CHUNK, body, 0)


def _mix_fwd(x, mod, g_pre, w_main, w_lr, w_g, b_g, w_s, bs_full, g_vn, g_out_a):
    bsz, seq, _ = x.shape
    tt = min(MIX_TILE, seq)
    n_t = seq // tt
    tok = lambda w: pl.BlockSpec((1, tt, w), lambda b, t: (b, t, 0))
    out_w = (SG_WIDTH, 2 * GLA_K_WIDTH, GLA_V_WIDTH, GLA_V_WIDTH, LR_PAD, GLA_V_WIDTH)
    return pl.pallas_call(
        _mix_fwd_kernel,
        grid=(bsz, n_t),
        in_specs=[
            tok(D_MODEL),
            pl.BlockSpec((1, N_MOD, D_MODEL), lambda b, t: (b, 0, 0)),
            _const_spec((1, D_MODEL)),
            _const_spec((D_MODEL, COL_LR)),
            _const_spec((D_MODEL, LR_PAD)),
            _const_spec((LR_PAD, 2 * GLA_K_WIDTH)),
            _const_spec((1, 2 * GLA_K_WIDTH)),
            _const_spec((SG_HEADS, SG_CHUNK, SG_CHUNK)),
            _const_spec((SG_HEADS, SG_CHUNK, SG_HEAD_DIM)),
            _const_spec((1, SG_WIDTH)),
            _const_spec((1, SG_WIDTH)),
        ],
        out_specs=[tok(w) for w in out_w],
        out_shape=[jax.ShapeDtypeStruct((bsz, seq, w), BF16) for w in out_w],
        scratch_shapes=[
            pltpu.VMEM((GLA_K_WIDTH, GLA_DV), F32),
            pltpu.VMEM((tt, 2 * GLA_K_WIDTH), F32),
            pltpu.VMEM((tt, GLA_K_WIDTH), F32),
        ],
        compiler_params=pltpu.CompilerParams(
            dimension_semantics=("arbitrary", "arbitrary"), vmem_limit_bytes=VMEM_LIMIT),
        name="mix_fwd",
    )(x, mod, g_pre, w_main, w_lr, w_g, b_g, w_s, bs_full, g_vn, g_out_a)


def _mix_bwd_kernel(x_ref, mod_ref, a_ref, qk_ref, v_ref, r_ref, lr_ref, of_ref, wg_ref, bg_ref, gob_ref, wout_ref,
                    gpost_ref, y_ref, s_ref, gate_ref, o_ref):
    tt = x_ref.shape[1]

    @pl.when(pl.program_id(1) == 0)
    def _():
        s_ref[...] = jnp.zeros_like(s_ref)

    gate_ref[...] = jnp.dot(lr_ref[0], wg_ref[:, GLA_K_WIDTH:], preferred_element_type=F32) + bg_ref[:, GLA_K_WIDTH:]
    n_c = tt // GLA_CHUNK

    def body(j, carry):
        r0 = pl.multiple_of((n_c - 1 - j) * GLA_CHUNK, GLA_CHUNK)
        rows = pl.ds(r0, GLA_CHUNK)
        qk = qk_ref[0, rows, :].astype(F32)
        o = _gla_chunk(qk[:, :GLA_K_WIDTH], qk[:, GLA_K_WIDTH:], v_ref[0, rows, :], gate_ref[rows, :], s_ref, True)
        o_ref[rows, :] = o + of_ref[0, rows, :].astype(F32)
        return carry

    lax.fori_loop(0, n_c, body, 0)

    r = r_ref[0].astype(F32)
    swish = r * _sigmoid(r)
    o_heads = []
    for hd in range(GLA_HEADS):
        vs = slice(hd * GLA_DV, (hd + 1) * GLA_DV)
        o_heads.append((_rms_scale(o_ref[:, vs]) * gob_ref[:, vs] * swish[:, vs]).astype(BF16))
    cat = jnp.concatenate([a_ref[0]] + o_heads, axis=1)
    y = jnp.dot(cat, wout_ref[...], preferred_element_type=F32)
    mod = mod_ref[0]
    y_ref[0] = x_ref[0] + mod[2:3] * (_rms_scale(y) * gpost_ref[...])


def _mix_bwd(x, mod, a, qk, v, r, lr, o_fwd, w_g, b_g, g_out_b, w_out, g_post):
    bsz, seq, _ = x.shape
    tt = min(MIX_TILE, seq)
    n_t = seq // tt
    tok = lambda w: pl.BlockSpec((1, tt, w), lambda b, t: (b, n_t - 1 - t, 0))
    return pl.pallas_call(
        _mix_bwd_kernel,
        grid=(bsz, n_t),
        in_specs=[
            tok(D_MODEL),
            pl.BlockSpec((1, N_MOD, D_MODEL), lambda b, t: (b, 0, 0)),
            tok(SG_WIDTH), tok(2 * GLA_K_WIDTH), tok(GLA_V_WIDTH), tok(GLA_V_WIDTH), tok(LR_PAD), tok(GLA_V_WIDTH),
            _const_spec((LR_PAD, 2 * GLA_K_WIDTH)),
            _const_spec((1, 2 * GLA_K_WIDTH)),
            _const_spec((1, GLA_V_WIDTH)),
            _const_spec((D_MODEL, D_MODEL)),
            _const_spec((1, D_MODEL)),
        ],
        out_specs=tok(D_MODEL),
        out_shape=jax.ShapeDtypeStruct((bsz, seq, D_MODEL), F32),
        scratch_shapes=[
            pltpu.VMEM((GLA_K_WIDTH, GLA_DV), F32),
            pltpu.VMEM((tt, GLA_K_WIDTH), F32),
            pltpu.VMEM((tt, GLA_V_WIDTH), F32),
        ],
        compiler_params=pltpu.CompilerParams(
            dimension_semantics=("arbitrary", "arbitrary"), vmem_limit_bytes=VMEM_LIMIT),
        name="mix_bwd",
    )(x, mod, a, qk, v, r, lr, o_fwd, w_g, b_g, g_out_b, w_out, g_post)


def _ffn_kernel(x_ref, xp_ref, xn_ref, mod_ref, gpre_ref, wup_ref, wconv_ref, bconv_ref, wdown_ref, gpost_ref,
                y_ref, up_ref, act_ref):
    tt = x_ref.shape[1]
    t = pl.program_id(1)
    n_t = pl.num_programs(1)
    mod = mod_ref[0]
    x = x_ref[0]
    xin = jnp.concatenate([xp_ref[0], x, xn_ref[0]], axis=0)
    h = _rms_scale(xin) * gpre_ref[...] * (1.0 + mod[4:5]) + mod[3:4]
    row = lax.broadcasted_iota(jnp.int32, (tt + 2 * CONV_HALO, 1), 0)
    inside = jnp.logical_and(jnp.logical_or(t > 0, row >= CONV_HALO),
                             jnp.logical_or(t < n_t - 1, row < tt + CONV_HALO))
    hb = jnp.where(inside, h, 0.0).astype(BF16)
    for j in range(D_FF // FF_BLOCK):
        for part in range(2):
            c0 = part * D_FF + j * FF_BLOCK
            cols = slice(c0, c0 + FF_BLOCK)
            up_ref[:, part * FF_BLOCK:(part + 1) * FF_BLOCK] = jnp.dot(hb, wup_ref[:, cols], preferred_element_type=F32)
        conv = []
        for part in range(2):
            c0 = part * D_FF + j * FF_BLOCK
            cols = slice(c0, c0 + FF_BLOCK)
            ps = slice(part * FF_BLOCK, (part + 1) * FF_BLOCK)
            conv.append(up_ref[pl.ds(CONV_HALO - 1, tt), ps] * wconv_ref[0:1, cols]
                        + up_ref[pl.ds(CONV_HALO, tt), ps] * wconv_ref[1:2, cols]
                        + up_ref[pl.ds(CONV_HALO + 1, tt), ps] * wconv_ref[2:3, cols]
                        + bconv_ref[:, cols])
        gate, val = conv
        act_ref[:, j * FF_BLOCK:(j + 1) * FF_BLOCK] = (gate * _sigmoid(gate) * val).astype(BF16)
    out = jnp.dot(act_ref[...], wdown_ref[...], preferred_element_type=F32)
    y_ref[0] = x + mod[5:6] * (_rms_scale(out) * gpost_ref[...])


def _ffn(x, mod, g_pre, w_up, w_conv, b_conv, w_down, g_post):
    bsz, seq, _ = x.shape
    tt = min(FFN_TILE, seq)
    n_t = seq // tt
    per = tt // CONV_HALO
    n_halo = seq // CONV_HALO
    return pl.pallas_call(
        _ffn_kernel,
        grid=(bsz, n_t),
        in_specs=[
            pl.BlockSpec((1, tt, D_MODEL), lambda b, t: (b, t, 0)),
            pl.BlockSpec((1, CONV_HALO, D_MODEL), lambda b, t: (b, jnp.maximum(t * per - 1, 0), 0)),
            pl.BlockSpec((1, CONV_HALO, D_MODEL), lambda b, t: (b, jnp.minimum((t + 1) * per, n_halo - 1), 0)),
            pl.BlockSpec((1, N_MOD, D_MODEL), lambda b, t: (b, 0, 0)),
            _const_spec((1, D_MODEL)),
            _const_spec((D_MODEL, 2 * D_FF)),
            _const_spec((3, 2 * D_FF)),
            _const_spec((1, 2 * D_FF)),
            _const_spec((D_FF, D_MODEL)),
            _const_spec((1, D_MODEL)),
        ],
        out_specs=pl.BlockSpec((1, tt, D_MODEL), lambda b, t: (b, t, 0)),
        out_shape=jax.ShapeDtypeStruct((bsz, seq, D_MODEL), F32),
        scratch_shapes=[
            pltpu.VMEM((tt + 2 * CONV_HALO, 2 * FF_BLOCK), F32),
            pltpu.VMEM((tt, D_FF), BF16),
        ],
        compiler_params=pltpu.CompilerParams(
            dimension_semantics=("arbitrary", "arbitrary"), vmem_limit_bytes=VMEM_LIMIT),
        name="ffn",
    )(x, x, x, mod, g_pre, w_up, w_conv, b_conv, w_down, g_post)


def _layer_params(l, g_pre_mix, g_post_mix, g_pre_ffn, g_post_ffn, w_in, w_s, b_s, g_vn, g_out_a, w_gf, b_gf, w_gb,
                  b_gb, g_out_b, w_out, w_up, w_conv, b_conv, w_down):
    row = lambda a: a[l].reshape(1, -1)
    w_lr = jnp.pad(w_in[l][:, COL_LR:COL_END], ((0, 0), (0, LR_PAD - (COL_END - COL_LR)))).astype(BF16)
    w_g = jnp.zeros((LR_PAD, 2 * GLA_K_WIDTH), F32)
    w_g = w_g.at[:GLA_GATE_RANK, :GLA_K_WIDTH].set(w_gf[l])
    w_g = w_g.at[GLA_GATE_RANK:2 * GLA_GATE_RANK, GLA_K_WIDTH:].set(w_gb[l]).astype(BF16)
    b_g = jnp.concatenate([b_gf[l], b_gb[l]]).reshape(1, -1)
    bs_full = jnp.broadcast_to(b_s[l][:, :, None], (SG_HEADS, SG_CHUNK, SG_HEAD_DIM))
    return dict(
        g_pre_mix=row(g_pre_mix), g_post_mix=row(g_post_mix), g_pre_ffn=row(g_pre_ffn), g_post_ffn=row(g_post_ffn),
        w_main=w_in[l][:, :COL_LR].astype(BF16), w_lr=w_lr, w_g=w_g, b_g=b_g,
        w_s=w_s[l].astype(BF16), bs_full=bs_full, g_vn=row(g_vn), g_out_a=row(g_out_a), g_out_b=row(g_out_b),
        w_out=w_out[l].astype(BF16), w_up=w_up[l].astype(BF16), w_conv=w_conv[l], b_conv=row(b_conv),
        w_down=w_down[l].astype(BF16))


def _layer(x, mod, p):
    a, qk, v, r, lr, o_fwd = _mix_fwd(x, mod, p["g_pre_mix"], p["w_main"], p["w_lr"], p["w_g"], p["b_g"], p["w_s"],
                                      p["bs_full"], p["g_vn"], p["g_out_a"])
    x = _mix_bwd(x, mod, a, qk, v, r, lr, o_fwd, p["w_g"], p["b_g"], p["g_out_b"], p["w_out"], p["g_post_mix"])
    return _ffn(x, mod, p["g_pre_ffn"], p["w_up"], p["w_conv"], p["b_conv"], p["w_down"], p["g_post_ffn"])


def _trunks(xs, cs, w_ada, b_ada, *weights):
    depth = w_ada.shape[0]
    sizes = [c.shape[0] for c in cs]
    mod_all = _modulation(jnp.concatenate(cs, axis=0), w_ada, b_ada)
    xs = list(xs)
    for l in range(depth):
        p = _layer_params(l, *weights)
        start = 0
        for i, n in enumerate(sizes):
            mod = mod_all[l, start:start + n].reshape(n, N_MOD, D_MODEL)
            xs[i] = _layer(xs[i], mod, p)
            start += n
    return tuple(xs)


def kernel(x_prompt, x_sample, c_prompt, c_sample, w_ada, b_ada, g_pre_mix, g_post_mix, g_pre_ffn, g_post_ffn, w_in, w_s, b_s, g_vn, g_out_a, w_gf, b_gf, w_gb, b_gb, g_out_b, w_out, w_up, w_conv, b_conv, w_down):
    return _trunks((x_prompt, x_sample), (c_prompt, c_sample), w_ada, b_ada, g_pre_mix, g_post_mix, g_pre_ffn,
                   g_post_ffn, w_in, w_s, b_s, g_vn, g_out_a, w_gf, b_gf, w_gb, b_gb, g_out_b, w_out, w_up, w_conv,
                   b_conv, w_down)
```

```python
import functools

import jax
import jax.numpy as jnp
from jax import lax
from jax.experimental import pallas as pl
from jax.experimental.pallas import tpu as pltpu

F32 = jnp.float32
BF16 = jnp.bfloat16

D_MODEL = 1024
N_MOD = 6
EPS = 1e-6
SG_HEADS = 4
SG_WIDTH = 512
SG_HEAD_DIM = 128
SG_CHUNK = 128
GLA_HEADS = 4
GLA_DK = 64
GLA_DV = 128
GLA_K_WIDTH = 256
GLA_V_WIDTH = 512
GLA_GATE_RANK = 16
GLA_GATE_NORMALIZER = 16.0
GLA_CHUNK = 128
COL_U, COL_VA, COL_Q, COL_V, COL_R, COL_LR, COL_END = 0, 512, 1024, 1536, 2048, 2560, 2592
LR_PAD = 128
D_FF = 2816
FF_BLOCK = 256
CONV_HALO = 8

MIX_TILE = 512
FFN_TILE = 512
VMEM_LIMIT = 56 * 1024 * 1024


def _sigmoid(x):
    return 1.0 / (1.0 + jnp.exp(-x))


def _log_sigmoid(x):
    return jnp.minimum(x, 0.0) - jnp.log(1.0 + jnp.exp(-jnp.abs(x)))


def _rms_scale(x):
    return x * lax.rsqrt(jnp.mean(x * x, axis=-1, keepdims=True) + EPS)


def _const_spec(shape):
    nd = len(shape)
    return pl.BlockSpec(shape, lambda *_: (0,) * nd, pipeline_mode=pl.Buffered(1))


def _mod_kernel(c_ref, w_ref, b_ref, o_ref):
    c = c_ref[...]
    s = (c * _sigmoid(c)).astype(BF16)
    o_ref[0] = jnp.dot(s, w_ref[0].astype(BF16), preferred_element_type=F32) + b_ref[0]


def _modulation(c_all, w_ada, b_ada):
    depth = w_ada.shape[0]
    rows = c_all.shape[0]
    cols = w_ada.shape[2]
    bn = 1536
    return pl.pallas_call(
        _mod_kernel,
        grid=(depth, cols // bn),
        in_specs=[
            pl.BlockSpec((rows, D_MODEL), lambda l, j: (0, 0)),
            pl.BlockSpec((1, D_MODEL, bn), lambda l, j: (l, 0, j)),
            pl.BlockSpec((1, 1, bn), lambda l, j: (l, 0, j)),
        ],
        out_specs=pl.BlockSpec((1, rows, bn), lambda l, j: (l, 0, j)),
        out_shape=jax.ShapeDtypeStruct((depth, rows, cols), F32),
        compiler_params=pltpu.CompilerParams(vmem_limit_bytes=VMEM_LIMIT),
        name="adaln_mod",
    )(c_all, w_ada, b_ada.reshape(depth, 1, cols))


def _gla_tile(qk_ref, v_ref, gate_ref, s_ref, write_o, reverse):
    tt = gate_ref.shape[0]
    c = GLA_CHUNK
    pk, pv = 2 * GLA_DK, 2 * GLA_DV
    row = lax.broadcasted_iota(jnp.int32, (c, c), 0)
    col = lax.broadcasted_iota(jnp.int32, (c, c), 1)
    causal = (col >= row) if reverse else (col <= row)
    tri = jnp.where(causal, 1.0, 0.0).astype(BF16)
    causal2 = jnp.concatenate([causal, causal], axis=1)
    first_k = lax.broadcasted_iota(jnp.int32, (c, pk), 1) < GLA_DK
    first_v = lax.broadcasted_iota(jnp.int32, (c, pv), 1) < GLA_DV
    diag = (lax.broadcasted_iota(jnp.int32, (pk, pv), 0) < GLA_DK) == (lax.broadcasted_iota(jnp.int32, (pk, pv), 1) < GLA_DV)
    rep_row = lax.broadcasted_iota(jnp.int32, (16, GLA_K_WIDTH), 0)
    ones16 = jnp.ones((16, pv), BF16)
    zero_k = jnp.zeros((c, pk), BF16)
    zero_v = jnp.zeros((c, pv), BF16)
    tn = (((0,), (0,)), ((), ()))
    nt = (((1,), (1,)), ((), ()))
    state = [s_ref[0], s_ref[1]]
    n_c = tt // c
    for ci in (range(n_c - 1, -1, -1) if reverse else range(n_c)):
        rows = slice(ci * c, (ci + 1) * c)
        g = _log_sigmoid(gate_ref[rows, :]) * (1.0 / GLA_GATE_NORMALIZER)
        g_hi = g.astype(BF16)
        g_lo = (g - g_hi.astype(F32)).astype(BF16)
        bb = jnp.dot(tri, jnp.concatenate([g_hi, g_lo], axis=1), preferred_element_type=F32)
        b = bb[:, :GLA_K_WIDTH] + bb[:, GLA_K_WIDTH:]
        b_mid = b[c // 2:c // 2 + 1] if reverse else b[c // 2 - 1:c // 2]
        b_last = b[0:1] if reverse else b[c - 1:c]
        qk = qk_ref[rows, :].astype(F32)
        q = qk[:, :GLA_K_WIDTH] * (GLA_DK ** -0.5)
        k = qk[:, GLA_K_WIDTH:]
        q_in = (q * jnp.exp(b)).astype(BF16)
        q_dec = (q * jnp.exp(b - b_mid)).astype(BF16)
        k_dec = (k * jnp.exp(b_mid - b)).astype(BF16)
        k_end = (k * jnp.exp(b_last - b)).astype(BF16)
        bl_hi = b_last.astype(BF16).astype(F32)
        rep = jnp.where(rep_row == 0, bl_hi, jnp.where(rep_row == 1, b_last - bl_hi, 0.0)).astype(BF16)
        v = v_ref[rows, :]
        outs = []
        for p in range(GLA_HEADS // 2):
            ks = slice(p * pk, (p + 1) * pk)
            kp = k_dec[:, ks]
            vp = v[:, p * pv:(p + 1) * pv]
            k_bd = jnp.concatenate([jnp.where(first_k, kp, zero_k), jnp.where(first_k, zero_k, kp)], axis=0)
            v_bd = jnp.concatenate([jnp.where(first_v, vp, zero_v), jnp.where(first_v, zero_v, vp)], axis=0)
            scores = lax.dot_general(q_dec[:, ks], k_bd, nt, preferred_element_type=F32)
            scores = jnp.where(causal2, scores, 0.0).astype(BF16)
            o = jnp.dot(scores, v_bd, preferred_element_type=F32)
            o = o + jnp.dot(q_in[:, ks], state[p].astype(BF16), preferred_element_type=F32)
            d_s = lax.dot_general(k_end[:, ks], vp, tn, preferred_element_type=F32)
            decay = jnp.exp(lax.dot_general(rep[:, ks], ones16, tn, preferred_element_type=F32))
            state[p] = decay * state[p] + jnp.where(diag, d_s, 0.0)
            outs.append(o)
        write_o(rows, jnp.concatenate(outs, axis=1))
    s_ref[0] = state[0]
    s_ref[1] = state[1]


def _mix_fwd_kernel(x_ref, mod_ref, gpre_ref, wmain_ref, wlr_ref, wg_ref, bg_ref, ws_ref, bs_ref, gvn_ref, goa_ref,
                    a_ref, qk_ref, v_ref, r_ref, lr_ref, of_ref,
                    s_ref, qkf_ref, gate_ref):
    tt = x_ref.shape[1]

    @pl.when(pl.program_id(1) == 0)
    def _():
        s_ref[...] = jnp.zeros_like(s_ref)

    x = x_ref[0]
    mod = mod_ref[0]
    h = _rms_scale(x) * gpre_ref[...] * (1.0 + mod[1:2]) + mod[0:1]
    hb = h.astype(BF16)
    proj = jnp.dot(hb, wmain_ref[...], preferred_element_type=F32)
    lr = jnp.dot(hb, wlr_ref[...], preferred_element_type=F32).astype(BF16)
    lr_ref[0] = lr
    gate_ref[...] = jnp.dot(lr, wg_ref[:, :GLA_K_WIDTH], preferred_element_type=F32) + bg_ref[:, :GLA_K_WIDTH]
    qkf_ref[...] = proj[:, COL_Q:COL_V]
    qk_ref[0] = proj[:, COL_Q:COL_V].astype(BF16)
    v_ref[0] = proj[:, COL_V:COL_R].astype(BF16)
    r_ref[0] = proj[:, COL_R:COL_LR].astype(BF16)

    n_sg = tt // SG_CHUNK
    a_heads = []
    for hd in range(SG_HEADS):
        cs = slice(hd * SG_HEAD_DIM, (hd + 1) * SG_HEAD_DIM)
        vh = proj[:, COL_VA + hd * SG_HEAD_DIM:COL_VA + (hd + 1) * SG_HEAD_DIM]
        xc = vh - jnp.mean(vh, axis=-1, keepdims=True)
        vn = (xc * lax.rsqrt(jnp.mean(xc * xc, axis=-1, keepdims=True) + EPS) * gvn_ref[:, cs]).astype(BF16)
        vn_cat = jnp.concatenate([vn[i * SG_CHUNK:(i + 1) * SG_CHUNK] for i in range(n_sg)], axis=1)
        mixed_cat = jnp.dot(ws_ref[hd], vn_cat, preferred_element_type=F32)
        mixed = jnp.concatenate([mixed_cat[:, i * SG_CHUNK:(i + 1) * SG_CHUNK] + bs_ref[hd] for i in range(n_sg)], axis=0)
        ah = proj[:, COL_U + hd * SG_HEAD_DIM:COL_U + (hd + 1) * SG_HEAD_DIM] * mixed
        a_heads.append((_rms_scale(ah) * goa_ref[:, cs]).astype(BF16))
    a_ref[0] = jnp.concatenate(a_heads, axis=1)

    def write_o(rows, o):
        of_ref[0, rows, :] = o.astype(of_ref.dtype)

    _gla_tile(qkf_ref, v_ref.at[0], gate_ref, s_ref, write_o, False)


def _mix_fwd(x, mod, g_pre, w_main, w_lr, w_g, b_g, w_s, bs_full, g_vn, g_out_a):
    bsz, seq, _ = x.shape
    tt = min(MIX_TILE, seq)
    n_t = seq // tt
    tok = lambda w: pl.BlockSpec((1, tt, w), lambda b, t: (b, t, 0))
    out_w = (SG_WIDTH, 2 * GLA_K_WIDTH, GLA_V_WIDTH, GLA_V_WIDTH, LR_PAD, GLA_V_WIDTH)
    return pl.pallas_call(
        _mix_fwd_kernel,
        grid=(bsz, n_t),
        in_specs=[
            tok(D_MODEL),
            pl.BlockSpec((1, N_MOD, D_MODEL), lambda b, t: (b, 0, 0)),
            _const_spec((1, D_MODEL)),
            _const_spec((D_MODEL, COL_LR)),
            _const_spec((D_MODEL, LR_PAD)),
            _const_spec((LR_PAD, 2 * GLA_K_WIDTH)),
            _const_spec((1, 2 * GLA_K_WIDTH)),
            _const_spec((SG_HEADS, SG_CHUNK, SG_CHUNK)),
            _const_spec((SG_HEADS, SG_CHUNK, SG_HEAD_DIM)),
            _const_spec((1, SG_WIDTH)),
            _const_spec((1, SG_WIDTH)),
        ],
        out_specs=[tok(w) for w in out_w],
        out_shape=[jax.ShapeDtypeStruct((bsz, seq, w), BF16) for w in out_w],
        scratch_shapes=[
            pltpu.VMEM((GLA_HEADS // 2, 2 * GLA_DK, 2 * GLA_DV), F32),
            pltpu.VMEM((tt, 2 * GLA_K_WIDTH), F32),
            pltpu.VMEM((tt, GLA_K_WIDTH), F32),
        ],
        compiler_params=pltpu.CompilerParams(
            dimension_semantics=("arbitrary", "arbitrary"), vmem_limit_bytes=VMEM_LIMIT),
        name="mix_fwd",
    )(x, mod, g_pre, w_main, w_lr, w_g, b_g, w_s, bs_full, g_vn, g_out_a)


def _mix_bwd_kernel(x_ref, mod_ref, a_ref, qk_ref, v_ref, r_ref, lr_ref, of_ref, wg_ref, bg_ref, gob_ref, wout_ref,
                    gpost_ref, y_ref, s_ref, gate_ref, o_ref):
    tt = x_ref.shape[1]

    @pl.when(pl.program_id(1) == 0)
    def _():
        s_ref[...] = jnp.zeros_like(s_ref)

    gate_ref[...] = jnp.dot(lr_ref[0], wg_ref[:, GLA_K_WIDTH:], preferred_element_type=F32) + bg_ref[:, GLA_K_WIDTH:]

    def write_o(rows, o):
        o_ref[rows, :] = o + of_ref[0, rows, :].astype(F32)

    _gla_tile(qk_ref.at[0], v_ref.at[0], gate_ref, s_ref, write_o, True)

    r = r_ref[0].astype(F32)
    swish = r * _sigmoid(r)
    o_heads = []
    for hd in range(GLA_HEADS):
        vs = slice(hd * GLA_DV, (hd + 1) * GLA_DV)
        o_heads.append((_rms_scale(o_ref[:, vs]) * gob_ref[:, vs] * swish[:, vs]).astype(BF16))
    cat = jnp.concatenate([a_ref[0]] + o_heads, axis=1)
    y = jnp.dot(cat, wout_ref[...], preferred_element_type=F32)
    mod = mod_ref[0]
    y_ref[0] = x_ref[0] + mod[2:3] * (_rms_scale(y) * gpost_ref[...])


def _mix_bwd(x, mod, a, qk, v, r, lr, o_fwd, w_g, b_g, g_out_b, w_out, g_post):
    bsz, seq, _ = x.shape
    tt = min(MIX_TILE, seq)
    n_t = seq // tt
    tok = lambda w: pl.BlockSpec((1, tt, w), lambda b, t: (b, n_t - 1 - t, 0))
    return pl.pallas_call(
        _mix_bwd_kernel,
        grid=(bsz, n_t),
        in_specs=[
            tok(D_MODEL),
            pl.BlockSpec((1, N_MOD, D_MODEL), lambda b, t: (b, 0, 0)),
            tok(SG_WIDTH), tok(2 * GLA_K_WIDTH), tok(GLA_V_WIDTH), tok(GLA_V_WIDTH), tok(LR_PAD), tok(GLA_V_WIDTH),
            _const_spec((LR_PAD, 2 * GLA_K_WIDTH)),
            _const_spec((1, 2 * GLA_K_WIDTH)),
            _const_spec((1, GLA_V_WIDTH)),
            _const_spec((D_MODEL, D_MODEL)),
            _const_spec((1, D_MODEL)),
        ],
        out_specs=tok(D_MODEL),
        out_shape=jax.ShapeDtypeStruct((bsz, seq, D_MODEL), F32),
        scratch_shapes=[
            pltpu.VMEM((GLA_HEADS // 2, 2 * GLA_DK, 2 * GLA_DV), F32),
            pltpu.VMEM((tt, GLA_K_WIDTH), F32),
            pltpu.VMEM((tt, GLA_V_WIDTH), F32),
        ],
        compiler_params=pltpu.CompilerParams(
            dimension_semantics=("arbitrary", "arbitrary"), vmem_limit_bytes=VMEM_LIMIT),
        name="mix_bwd",
    )(x, mod, a, qk, v, r, lr, o_fwd, w_g, b_g, g_out_b, w_out, g_post)


def _ffn_kernel(x_ref, xp_ref, xn_ref, mod_ref, gpre_ref, wup_ref, wconv_ref, bconv_ref, wdown_ref, gpost_ref,
                y_ref, up_ref, act_ref):
    tt = x_ref.shape[1]
    t = pl.program_id(1)
    n_t = pl.num_programs(1)
    mod = mod_ref[0]
    x = x_ref[0]
    xin = jnp.concatenate([xp_ref[0], x, xn_ref[0]], axis=0)
    h = _rms_scale(xin) * gpre_ref[...] * (1.0 + mod[4:5]) + mod[3:4]
    row = lax.broadcasted_iota(jnp.int32, (tt + 2 * CONV_HALO, 1), 0)
    inside = jnp.logical_and(jnp.logical_or(t > 0, row >= CONV_HALO),
                             jnp.logical_or(t < n_t - 1, row < tt + CONV_HALO))
    hb = jnp.where(inside, h, 0.0).astype(BF16)
    for j in range(D_FF // FF_BLOCK):
        for part in range(2):
            c0 = part * D_FF + j * FF_BLOCK
            cols = slice(c0, c0 + FF_BLOCK)
            up_ref[:, part * FF_BLOCK:(part + 1) * FF_BLOCK] = jnp.dot(hb, wup_ref[:, cols], preferred_element_type=F32)
        conv = []
        for part in range(2):
            c0 = part * D_FF + j * FF_BLOCK
            cols = slice(c0, c0 + FF_BLOCK)
            ps = slice(part * FF_BLOCK, (part + 1) * FF_BLOCK)
            conv.append(up_ref[pl.ds(CONV_HALO - 1, tt), ps] * wconv_ref[0:1, cols]
                        + up_ref[pl.ds(CONV_HALO, tt), ps] * wconv_ref[1:2, cols]
                        + up_ref[pl.ds(CONV_HALO + 1, tt), ps] * wconv_ref[2:3, cols]
                        + bconv_ref[:, cols])
        gate, val = conv
        act_ref[:, j * FF_BLOCK:(j + 1) * FF_BLOCK] = (gate * _sigmoid(gate) * val).astype(BF16)
    out = jnp.dot(act_ref[...], wdown_ref[...], preferred_element_type=F32)
    y_ref[0] = x + mod[5:6] * (_rms_scale(out) * gpost_ref[...])


def _ffn(x, mod, g_pre, w_up, w_conv, b_conv, w_down, g_post):
    bsz, seq, _ = x.shape
    tt = min(FFN_TILE, seq)
    n_t = seq // tt
    per = tt // CONV_HALO
    n_halo = seq // CONV_HALO
    return pl.pallas_call(
        _ffn_kernel,
        grid=(bsz, n_t),
        in_specs=[
            pl.BlockSpec((1, tt, D_MODEL), lambda b, t: (b, t, 0)),
            pl.BlockSpec((1, CONV_HALO, D_MODEL), lambda b, t: (b, jnp.maximum(t * per - 1, 0), 0)),
            pl.BlockSpec((1, CONV_HALO, D_MODEL), lambda b, t: (b, jnp.minimum((t + 1) * per, n_halo - 1), 0)),
            pl.BlockSpec((1, N_MOD, D_MODEL), lambda b, t: (b, 0, 0)),
            _const_spec((1, D_MODEL)),
            _const_spec((D_MODEL, 2 * D_FF)),
            _const_spec((3, 2 * D_FF)),
            _const_spec((1, 2 * D_FF)),
            _const_spec((D_FF, D_MODEL)),
            _const_spec((1, D_MODEL)),
        ],
        out_specs=pl.BlockSpec((1, tt, D_MODEL), lambda b, t: (b, t, 0)),
        out_shape=jax.ShapeDtypeStruct((bsz, seq, D_MODEL), F32),
        scratch_shapes=[
            pltpu.VMEM((tt + 2 * CONV_HALO, 2 * FF_BLOCK), F32),
            pltpu.VMEM((tt, D_FF), BF16),
        ],
        compiler_params=pltpu.CompilerParams(
            dimension_semantics=("arbitrary", "arbitrary"), vmem_limit_bytes=VMEM_LIMIT),
        name="ffn",
    )(x, x, x, mod, g_pre, w_up, w_conv, b_conv, w_down, g_post)


def _layer_params(l, g_pre_mix, g_post_mix, g_pre_ffn, g_post_ffn, w_in, w_s, b_s, g_vn, g_out_a, w_gf, b_gf, w_gb,
                  b_gb, g_out_b, w_out, w_up, w_conv, b_conv, w_down):
    row = lambda a: a[l].reshape(1, -1)
    w_lr = jnp.pad(w_in[l][:, COL_LR:COL_END], ((0, 0), (0, LR_PAD - (COL_END - COL_LR)))).astype(BF16)
    w_g = jnp.zeros((LR_PAD, 2 * GLA_K_WIDTH), F32)
    w_g = w_g.at[:GLA_GATE_RANK, :GLA_K_WIDTH].set(w_gf[l])
    w_g = w_g.at[GLA_GATE_RANK:2 * GLA_GATE_RANK, GLA_K_WIDTH:].set(w_gb[l]).astype(BF16)
    b_g = jnp.concatenate([b_gf[l], b_gb[l]]).reshape(1, -1)
    bs_full = jnp.broadcast_to(b_s[l][:, :, None], (SG_HEADS, SG_CHUNK, SG_HEAD_DIM))
    return dict(
        g_pre_mix=row(g_pre_mix), g_post_mix=row(g_post_mix), g_pre_ffn=row(g_pre_ffn), g_post_ffn=row(g_post_ffn),
        w_main=w_in[l][:, :COL_LR].astype(BF16), w_lr=w_lr, w_g=w_g, b_g=b_g,
        w_s=w_s[l].astype(BF16), bs_full=bs_full, g_vn=row(g_vn), g_out_a=row(g_out_a), g_out_b=row(g_out_b),
        w_out=w_out[l].astype(BF16), w_up=w_up[l].astype(BF16), w_conv=w_conv[l], b_conv=row(b_conv),
        w_down=w_down[l].astype(BF16))


def _layer(x, mod, p):
    a, qk, v, r, lr, o_fwd = _mix_fwd(x, mod, p["g_pre_mix"], p["w_main"], p["w_lr"], p["w_g"], p["b_g"], p["w_s"],
                                      p["bs_full"], p["g_vn"], p["g_out_a"])
    x = _mix_bwd(x, mod, a, qk, v, r, lr, o_fwd, p["w_g"], p["b_g"], p["g_out_b"], p["w_out"], p["g_post_mix"])
    return _ffn(x, mod, p["g_pre_ffn"], p["w_up"], p["w_conv"], p["b_conv"], p["w_down"], p["g_post_ffn"])


def _trunks(xs, cs, w_ada, b_ada, *weights):
    depth = w_ada.shape[0]
    sizes = [c.shape[0] for c in cs]
    mod_all = _modulation(jnp.concatenate(cs, axis=0), w_ada, b_ada)
    xs = list(xs)
    for l in range(depth):
        p = _layer_params(l, *weights)
        start = 0
        for i, n in enumerate(sizes):
            mod = mod_all[l, start:start + n].reshape(n, N_MOD, D_MODEL)
            xs[i] = _layer(xs[i], mod, p)
            start += n
    return tuple(xs)


def kernel(x_prompt, x_sample, c_prompt, c_sample, w_ada, b_ada, g_pre_mix, g_post_mix, g_pre_ffn, g_post_ffn, w_in, w_s, b_s, g_vn, g_out_a, w_gf, b_gf, w_gb, b_gb, g_out_b, w_out, w_up, w_conv, b_conv, w_down):
    return _trunks((x_prompt, x_sample), (c_prompt, c_sample), w_ada, b_ada, g_pre_mix, g_post_mix, g_pre_ffn,
                   g_post_ffn, w_in, w_s, b_s, g_vn, g_out_a, w_gf, b_gf, w_gb, b_gb, g_out_b, w_out, w_up, w_conv,
                   b_conv, w_down)
```

```python
import functools

import jax
import jax.numpy as jnp
from jax import lax
from jax.experimental import pallas as pl
from jax.experimental.pallas import tpu as pltpu

F32 = jnp.float32
BF16 = jnp.bfloat16

D_MODEL = 1024
N_MOD = 6
EPS = 1e-6
SG_HEADS = 4
SG_WIDTH = 512
SG_HEAD_DIM = 128
SG_CHUNK = 128
GLA_HEADS = 4
GLA_DK = 64
GLA_DV = 128
GLA_K_WIDTH = 256
GLA_V_WIDTH = 512
GLA_GATE_RANK = 16
GLA_GATE_NORMALIZER = 16.0
GLA_CHUNK = 128
COL_U, COL_VA, COL_Q, COL_V, COL_R, COL_LR, COL_END = 0, 512, 1024, 1536, 2048, 2560, 2592
LR_PAD = 128
D_FF = 2816
FF_BLOCK = 256
CONV_HALO = 8

MIX_TILE = 512
FFN_TILE = 512
VMEM_LIMIT = 56 * 1024 * 1024


def _sigmoid(x):
    return 1.0 / (1.0 + jnp.exp(-x))


def _log_sigmoid(x):
    return jnp.minimum(x, 0.0) - jnp.log(1.0 + jnp.exp(-jnp.abs(x)))


def _rms_scale(x):
    return x * lax.rsqrt(jnp.mean(x * x, axis=-1, keepdims=True) + EPS)


def _adaln(x, gain_row, scale_row, shift_row):
    return _rms_scale(x) * (gain_row * (1.0 + scale_row)) + shift_row


def _const_spec(shape):
    nd = len(shape)
    return pl.BlockSpec(shape, lambda *_: (0,) * nd, pipeline_mode=pl.Buffered(1))


def _mod_kernel(c_ref, w_ref, b_ref, o_ref):
    c = c_ref[...]
    s = (c * _sigmoid(c)).astype(BF16)
    o_ref[0] = jnp.dot(s, w_ref[0].astype(BF16), preferred_element_type=F32) + b_ref[0]


def _modulation(c_all, w_ada, b_ada):
    depth = w_ada.shape[0]
    rows = c_all.shape[0]
    cols = w_ada.shape[2]
    bn = 1536
    return pl.pallas_call(
        _mod_kernel,
        grid=(depth, cols // bn),
        in_specs=[
            pl.BlockSpec((rows, D_MODEL), lambda l, j: (0, 0)),
            pl.BlockSpec((1, D_MODEL, bn), lambda l, j: (l, 0, j)),
            pl.BlockSpec((1, 1, bn), lambda l, j: (l, 0, j)),
        ],
        out_specs=pl.BlockSpec((1, rows, bn), lambda l, j: (l, 0, j)),
        out_shape=jax.ShapeDtypeStruct((depth, rows, cols), F32),
        compiler_params=pltpu.CompilerParams(vmem_limit_bytes=VMEM_LIMIT),
        name="adaln_mod",
    )(c_all, w_ada, b_ada.reshape(depth, 1, cols))


def _gla_tile(qk_ref, v_ref, gate_ref, s_ref, write_o, reverse):
    tt = gate_ref.shape[0]
    c = GLA_CHUNK
    n_c = tt // c
    n_p = GLA_HEADS // 2
    kw = GLA_K_WIDTH
    pk, pv = 2 * GLA_DK, 2 * GLA_DV
    row = lax.broadcasted_iota(jnp.int32, (c, c), 0)
    col = lax.broadcasted_iota(jnp.int32, (c, c), 1)
    causal = (col >= row) if reverse else (col <= row)
    tri = jnp.where(causal, 1.0, 0.0).astype(BF16)
    causal2 = jnp.concatenate([causal, causal], axis=1)
    first_k = lax.broadcasted_iota(jnp.int32, (c, pk), 1) < GLA_DK
    first_v = lax.broadcasted_iota(jnp.int32, (c, pv), 1) < GLA_DV
    diag = (lax.broadcasted_iota(jnp.int32, (pk, pv), 0) < GLA_DK) == (lax.broadcasted_iota(jnp.int32, (pk, pv), 1) < GLA_DV)
    rep_row = lax.broadcasted_iota(jnp.int32, (16, kw), 0)
    ones16 = jnp.ones((16, pv), BF16)
    zero_k = jnp.zeros((c, pk), BF16)
    zero_v = jnp.zeros((c, pv), BF16)
    tn = (((0,), (0,)), ((), ()))
    nt = (((1,), (1,)), ((), ()))
    chunks = range(n_c)
    rows = [slice(ci * c, (ci + 1) * c) for ci in chunks]

    g = _log_sigmoid(gate_ref[...]) * (1.0 / GLA_GATE_NORMALIZER)
    g_hi = g.astype(BF16)
    g_lo = (g - g_hi.astype(F32)).astype(BF16)
    g_cat = jnp.concatenate([part[r] for r in rows for part in (g_hi, g_lo)], axis=1)
    bb = jnp.dot(tri, g_cat, preferred_element_type=F32)
    b = [bb[:, 2 * ci * kw:(2 * ci + 1) * kw] + bb[:, (2 * ci + 1) * kw:(2 * ci + 2) * kw] for ci in chunks]
    b_mid = [x[c // 2:c // 2 + 1] if reverse else x[c // 2 - 1:c // 2] for x in b]
    b_last = [x[0:1] if reverse else x[c - 1:c] for x in b]

    q_in, q_dec, k_dec, k_end, reps = [], [], [], [], []
    for ci in chunks:
        qk = qk_ref[rows[ci], :].astype(F32)
        q = qk[:, :kw] * (GLA_DK ** -0.5)
        k = qk[:, kw:]
        q_in.append((q * jnp.exp(b[ci])).astype(BF16))
        q_dec.append((q * jnp.exp(b[ci] - b_mid[ci])).astype(BF16))
        k_dec.append((k * jnp.exp(b_mid[ci] - b[ci])).astype(BF16))
        k_end.append((k * jnp.exp(b_last[ci] - b[ci])).astype(BF16))
        bl_hi = b_last[ci].astype(BF16).astype(F32)
        reps.append(jnp.where(rep_row == 0, bl_hi, jnp.where(rep_row == 1, b_last[ci] - bl_hi, 0.0)).astype(BF16))

    pairs = [(ci, p) for ci in chunks for p in range(n_p)]
    ks = [slice(p * pk, (p + 1) * pk) for p in range(n_p)]
    vp = {(ci, p): v_ref[rows[ci], p * pv:(p + 1) * pv] for ci, p in pairs}
    scores, d_s, decay = {}, {}, {}
    for ci, p in pairs:
        kp = k_dec[ci][:, ks[p]]
        k_bd = jnp.concatenate([jnp.where(first_k, kp, zero_k), jnp.where(first_k, zero_k, kp)], axis=0)
        scores[ci, p] = lax.dot_general(q_dec[ci][:, ks[p]], k_bd, nt, preferred_element_type=F32)
    for ci, p in pairs:
        d_s[ci, p] = lax.dot_general(k_end[ci][:, ks[p]], vp[ci, p], tn, preferred_element_type=F32)
        decay[ci, p] = lax.dot_general(reps[ci][:, ks[p]], ones16, tn, preferred_element_type=F32)
    o = {}
    for ci, p in pairs:
        v_bd = jnp.concatenate([jnp.where(first_v, vp[ci, p], zero_v), jnp.where(first_v, zero_v, vp[ci, p])], axis=0)
        masked = jnp.where(causal2, scores[ci, p], 0.0).astype(BF16)
        o[ci, p] = jnp.dot(masked, v_bd, preferred_element_type=F32)
    states = {}
    for p in range(n_p):
        s = s_ref[p]
        for ci in (reversed(chunks) if reverse else chunks):
            states[ci, p] = s.astype(BF16)
            s = jnp.exp(decay[ci, p]) * s + jnp.where(diag, d_s[ci, p], 0.0)
        s_ref[p] = s
    for ci in chunks:
        write_o(rows[ci], jnp.concatenate(
            [o[ci, p] + jnp.dot(q_in[ci][:, ks[p]], states[ci, p], preferred_element_type=F32) for p in range(n_p)],
            axis=1))


def _mix_fwd_kernel(x_ref, mod_ref, gpre_ref, wmain_ref, wlr_ref, wg_ref, bg_ref, ws_ref, bs_ref, gvn_ref, goa_ref,
                    a_ref, qk_ref, v_ref, r_ref, lr_ref, of_ref,
                    s_ref, qkf_ref, gate_ref):
    tt = x_ref.shape[1]

    @pl.when(pl.program_id(1) == 0)
    def _():
        s_ref[...] = jnp.zeros_like(s_ref)

    mod = mod_ref[0]
    hb = _adaln(x_ref[0], gpre_ref[...], mod[1:2], mod[0:1]).astype(BF16)
    proj = jnp.dot(hb, wmain_ref[...], preferred_element_type=F32)
    lr = jnp.dot(hb, wlr_ref[...], preferred_element_type=F32).astype(BF16)
    lr_ref[0] = lr
    gate_ref[...] = jnp.dot(lr, wg_ref[:, :GLA_K_WIDTH], preferred_element_type=F32) + bg_ref[:, :GLA_K_WIDTH]
    qkf_ref[...] = proj[:, COL_Q:COL_V]
    qk_ref[0] = proj[:, COL_Q:COL_V].astype(BF16)
    v_ref[0] = proj[:, COL_V:COL_R].astype(BF16)
    r_ref[0] = proj[:, COL_R:COL_LR].astype(BF16)

    n_sg = tt // SG_CHUNK
    heads = [slice(hd * SG_HEAD_DIM, (hd + 1) * SG_HEAD_DIM) for hd in range(SG_HEADS)]
    vn_cat = []
    for cs in heads:
        vh = proj[:, COL_VA + cs.start:COL_VA + cs.stop]
        xc = vh - jnp.mean(vh, axis=-1, keepdims=True)
        vn = (xc * lax.rsqrt(jnp.mean(xc * xc, axis=-1, keepdims=True) + EPS) * gvn_ref[:, cs]).astype(BF16)
        vn_cat.append(jnp.concatenate([vn[i * SG_CHUNK:(i + 1) * SG_CHUNK] for i in range(n_sg)], axis=1))
    mixed_cat = [jnp.dot(ws_ref[hd], vn_cat[hd], preferred_element_type=F32) for hd in range(SG_HEADS)]
    a_heads = []
    for hd, cs in enumerate(heads):
        mixed = jnp.concatenate(
            [mixed_cat[hd][:, i * SG_CHUNK:(i + 1) * SG_CHUNK] + bs_ref[hd] for i in range(n_sg)], axis=0)
        ah = proj[:, COL_U + cs.start:COL_U + cs.stop] * mixed
        a_heads.append((_rms_scale(ah) * goa_ref[:, cs]).astype(BF16))
    a_ref[0] = jnp.concatenate(a_heads, axis=1)

    def write_o(rows, o):
        of_ref[0, rows, :] = o.astype(of_ref.dtype)

    _gla_tile(qkf_ref, v_ref.at[0], gate_ref, s_ref, write_o, False)


def _mix_fwd(x, mod, g_pre, w_main, w_lr, w_g, b_g, w_s, bs_full, g_vn, g_out_a):
    bsz, seq, _ = x.shape
    tt = min(MIX_TILE, seq)
    n_t = seq // tt
    tok = lambda w: pl.BlockSpec((1, tt, w), lambda b, t: (b, t, 0))
    out_w = (SG_WIDTH, 2 * GLA_K_WIDTH, GLA_V_WIDTH, GLA_V_WIDTH, LR_PAD, GLA_V_WIDTH)
    return pl.pallas_call(
        _mix_fwd_kernel,
        grid=(bsz, n_t),
        in_specs=[
            tok(D_MODEL),
            pl.BlockSpec((1, N_MOD, D_MODEL), lambda b, t: (b, 0, 0)),
            _const_spec((1, D_MODEL)),
            _const_spec((D_MODEL, COL_LR)),
            _const_spec((D_MODEL, LR_PAD)),
            _const_spec((LR_PAD, 2 * GLA_K_WIDTH)),
            _const_spec((1, 2 * GLA_K_WIDTH)),
            _const_spec((SG_HEADS, SG_CHUNK, SG_CHUNK)),
            _const_spec((SG_HEADS, SG_CHUNK, SG_HEAD_DIM)),
            _const_spec((1, SG_WIDTH)),
            _const_spec((1, SG_WIDTH)),
        ],
        out_specs=[tok(w) for w in out_w],
        out_shape=[jax.ShapeDtypeStruct((bsz, seq, w), BF16) for w in out_w],
        scratch_shapes=[
            pltpu.VMEM((GLA_HEADS // 2, 2 * GLA_DK, 2 * GLA_DV), F32),
            pltpu.VMEM((tt, 2 * GLA_K_WIDTH), F32),
            pltpu.VMEM((tt, GLA_K_WIDTH), F32),
        ],
        compiler_params=pltpu.CompilerParams(
            dimension_semantics=("arbitrary", "arbitrary"), vmem_limit_bytes=VMEM_LIMIT),
        name="mix_fwd",
    )(x, mod, g_pre, w_main, w_lr, w_g, b_g, w_s, bs_full, g_vn, g_out_a)


def _mix_bwd_kernel(x_ref, mod_ref, a_ref, qk_ref, v_ref, r_ref, lr_ref, of_ref, wg_ref, bg_ref, gob_ref, wout_ref,
                    gpost_ref, y_ref, s_ref, gate_ref, o_ref):
    tt = x_ref.shape[1]

    @pl.when(pl.program_id(1) == 0)
    def _():
        s_ref[...] = jnp.zeros_like(s_ref)

    gate_ref[...] = jnp.dot(lr_ref[0], wg_ref[:, GLA_K_WIDTH:], preferred_element_type=F32) + bg_ref[:, GLA_K_WIDTH:]

    def write_o(rows, o):
        o_ref[rows, :] = o + of_ref[0, rows, :].astype(F32)

    _gla_tile(qk_ref.at[0], v_ref.at[0], gate_ref, s_ref, write_o, True)

    r = r_ref[0].astype(F32)
    swish = r * _sigmoid(r)
    o_heads = []
    for hd in range(GLA_HEADS):
        vs = slice(hd * GLA_DV, (hd + 1) * GLA_DV)
        o_heads.append((_rms_scale(o_ref[:, vs]) * (gob_ref[:, vs] * swish[:, vs])).astype(BF16))
    cat = jnp.concatenate([a_ref[0]] + o_heads, axis=1)
    y = jnp.dot(cat, wout_ref[...], preferred_element_type=F32)
    mod = mod_ref[0]
    y_ref[0] = x_ref[0] + _rms_scale(y) * (mod[2:3] * gpost_ref[...])


def _mix_bwd(x, mod, a, qk, v, r, lr, o_fwd, w_g, b_g, g_out_b, w_out, g_post):
    bsz, seq, _ = x.shape
    tt = min(MIX_TILE, seq)
    n_t = seq // tt
    tok = lambda w: pl.BlockSpec((1, tt, w), lambda b, t: (b, n_t - 1 - t, 0))
    return pl.pallas_call(
        _mix_bwd_kernel,
        grid=(bsz, n_t),
        in_specs=[
            tok(D_MODEL),
            pl.BlockSpec((1, N_MOD, D_MODEL), lambda b, t: (b, 0, 0)),
            tok(SG_WIDTH), tok(2 * GLA_K_WIDTH), tok(GLA_V_WIDTH), tok(GLA_V_WIDTH), tok(LR_PAD), tok(GLA_V_WIDTH),
            _const_spec((LR_PAD, 2 * GLA_K_WIDTH)),
            _const_spec((1, 2 * GLA_K_WIDTH)),
            _const_spec((1, GLA_V_WIDTH)),
            _const_spec((D_MODEL, D_MODEL)),
            _const_spec((1, D_MODEL)),
        ],
        out_specs=tok(D_MODEL),
        out_shape=jax.ShapeDtypeStruct((bsz, seq, D_MODEL), F32),
        scratch_shapes=[
            pltpu.VMEM((GLA_HEADS // 2, 2 * GLA_DK, 2 * GLA_DV), F32),
            pltpu.VMEM((tt, GLA_K_WIDTH), F32),
            pltpu.VMEM((tt, GLA_V_WIDTH), F32),
        ],
        compiler_params=pltpu.CompilerParams(
            dimension_semantics=("arbitrary", "arbitrary"), vmem_limit_bytes=VMEM_LIMIT),
        name="mix_bwd",
    )(x, mod, a, qk, v, r, lr, o_fwd, w_g, b_g, g_out_b, w_out, g_post)


def _ffn_kernel(x_ref, xp_ref, xn_ref, mod_ref, gpre_ref, wup_ref, wconv_ref, bconv_ref, wdown_ref, gpost_ref,
                y_ref, act_ref):
    tt = x_ref.shape[1]
    pitch = tt // 8
    t = pl.program_id(1)
    n_t = pl.num_programs(1)
    mod = mod_ref[0]
    xq = pltpu.einshape("smd->msd", x_ref[0].reshape(8, pitch, D_MODEL)).reshape(tt, D_MODEL)
    sub = lax.broadcasted_iota(jnp.int32, (8, 1), 0)
    halo = jnp.where(sub == 0, pltpu.roll(xp_ref[0], 1, 0), pltpu.roll(xn_ref[0], 7, 0))
    keep = jnp.logical_or(jnp.logical_and(sub == 0, t > 0), jnp.logical_and(sub == 7, t < n_t - 1))
    hq = _adaln(xq, gpre_ref[...], mod[4:5], mod[3:4]).astype(BF16)
    hh = jnp.where(keep, _adaln(halo, gpre_ref[...], mod[4:5], mod[3:4]), 0.0)
    hb = jnp.concatenate([hq, jnp.concatenate([hh, jnp.zeros_like(hh)], axis=0).astype(BF16)], axis=0)
    for j in range(D_FF // FF_BLOCK):
        conv = []
        for part in range(2):
            c0 = part * D_FF + j * FF_BLOCK
            cols = slice(c0, c0 + FF_BLOCK)
            u = jnp.dot(hb, wup_ref[:, cols], preferred_element_type=F32)
            edge = u[tt:tt + 8]
            before = jnp.concatenate([jnp.where(sub == 0, edge, pltpu.roll(u[tt - 8:tt], 1, 0)), u[0:tt - 8]], axis=0)
            after = jnp.concatenate([u[8:tt], jnp.where(sub == 7, edge, pltpu.roll(u[0:8], 7, 0))], axis=0)
            conv.append(before * wconv_ref[0:1, cols] + u[0:tt] * wconv_ref[1:2, cols] + after * wconv_ref[2:3, cols]
                        + bconv_ref[:, cols])
        gate, val = conv
        act_ref[:, j * FF_BLOCK:(j + 1) * FF_BLOCK] = (gate * _sigmoid(gate) * val).astype(BF16)
    out = jnp.dot(act_ref[...], wdown_ref[...], preferred_element_type=F32)
    yq = xq + _rms_scale(out) * (mod[5:6] * gpost_ref[...])
    y_ref[0] = pltpu.einshape("msd->smd", yq.reshape(pitch, 8, D_MODEL)).reshape(tt, D_MODEL)


def _ffn(x, mod, g_pre, w_up, w_conv, b_conv, w_down, g_post):
    bsz, seq, _ = x.shape
    tt = min(FFN_TILE, seq)
    n_t = seq // tt
    per = tt // CONV_HALO
    n_halo = seq // CONV_HALO
    return pl.pallas_call(
        _ffn_kernel,
        grid=(bsz, n_t),
        in_specs=[
            pl.BlockSpec((1, tt, D_MODEL), lambda b, t: (b, t, 0)),
            pl.BlockSpec((1, CONV_HALO, D_MODEL), lambda b, t: (b, jnp.maximum(t * per - 1, 0), 0)),
            pl.BlockSpec((1, CONV_HALO, D_MODEL), lambda b, t: (b, jnp.minimum((t + 1) * per, n_halo - 1), 0)),
            pl.BlockSpec((1, N_MOD, D_MODEL), lambda b, t: (b, 0, 0)),
            _const_spec((1, D_MODEL)),
            _const_spec((D_MODEL, 2 * D_FF)),
            _const_spec((3, 2 * D_FF)),
            _const_spec((1, 2 * D_FF)),
            _const_spec((D_FF, D_MODEL)),
            _const_spec((1, D_MODEL)),
        ],
        out_specs=pl.BlockSpec((1, tt, D_MODEL), lambda b, t: (b, t, 0)),
        out_shape=jax.ShapeDtypeStruct((bsz, seq, D_MODEL), F32),
        scratch_shapes=[pltpu.VMEM((tt, D_FF), BF16)],
        compiler_params=pltpu.CompilerParams(
            dimension_semantics=("arbitrary", "arbitrary"), vmem_limit_bytes=VMEM_LIMIT),
        name="ffn",
    )(x, x, x, mod, g_pre, w_up, w_conv, b_conv, w_down, g_post)


def _layer_params(l, g_pre_mix, g_post_mix, g_pre_ffn, g_post_ffn, w_in, w_s, b_s, g_vn, g_out_a, w_gf, b_gf, w_gb,
                  b_gb, g_out_b, w_out, w_up, w_conv, b_conv, w_down):
    row = lambda a: a[l].reshape(1, -1)
    w_lr = jnp.pad(w_in[l][:, COL_LR:COL_END], ((0, 0), (0, LR_PAD - (COL_END - COL_LR)))).astype(BF16)
    w_g = jnp.zeros((LR_PAD, 2 * GLA_K_WIDTH), F32)
    w_g = w_g.at[:GLA_GATE_RANK, :GLA_K_WIDTH].set(w_gf[l])
    w_g = w_g.at[GLA_GATE_RANK:2 * GLA_GATE_RANK, GLA_K_WIDTH:].set(w_gb[l]).astype(BF16)
    b_g = jnp.concatenate([b_gf[l], b_gb[l]]).reshape(1, -1)
    bs_full = jnp.broadcast_to(b_s[l][:, :, None], (SG_HEADS, SG_CHUNK, SG_HEAD_DIM))
    return dict(
        g_pre_mix=row(g_pre_mix), g_post_mix=row(g_post_mix), g_pre_ffn=row(g_pre_ffn), g_post_ffn=row(g_post_ffn),
        w_main=w_in[l][:, :COL_LR].astype(BF16), w_lr=w_lr, w_g=w_g, b_g=b_g,
        w_s=w_s[l].astype(BF16), bs_full=bs_full, g_vn=row(g_vn), g_out_a=row(g_out_a), g_out_b=row(g_out_b),
        w_out=w_out[l].astype(BF16), w_up=w_up[l].astype(BF16), w_conv=w_conv[l], b_conv=row(b_conv),
        w_down=w_down[l].astype(BF16))


def _layer(x, mod, p):
    a, qk, v, r, lr, o_fwd = _mix_fwd(x, mod, p["g_pre_mix"], p["w_main"], p["w_lr"], p["w_g"], p["b_g"], p["w_s"],
                                      p["bs_full"], p["g_vn"], p["g_out_a"])
    x = _mix_bwd(x, mod, a, qk, v, r, lr, o_fwd, p["w_g"], p["b_g"], p["g_out_b"], p["w_out"], p["g_post_mix"])
    return _ffn(x, mod, p["g_pre_ffn"], p["w_up"], p["w_conv"], p["b_conv"], p["w_down"], p["g_post_ffn"])


def _trunks(xs, cs, w_ada, b_ada, *weights):
    depth = w_ada.shape[0]
    sizes = [c.shape[0] for c in cs]
    mod_all = _modulation(jnp.concatenate(cs, axis=0), w_ada, b_ada)
    xs = list(xs)
    for l in range(depth):
        p = _layer_params(l, *weights)
        start = 0
        for i, n in enumerate(sizes):
            mod = mod_all[l, start:start + n].reshape(n, N_MOD, D_MODEL)
            xs[i] = _layer(xs[i], mod, p)
            start += n
    return tuple(xs)


def kernel(x_prompt, x_sample, c_prompt, c_sample, w_ada, b_ada, g_pre_mix, g_post_mix, g_pre_ffn, g_post_ffn, w_in, w_s, b_s, g_vn, g_out_a, w_gf, b_gf, w_gb, b_gb, g_out_b, w_out, w_up, w_conv, b_conv, w_down):
    return _trunks((x_prompt, x_sample), (c_prompt, c_sample), w_ada, b_ada, g_pre_mix, g_post_mix, g_pre_ffn,
                   g_post_ffn, w_in, w_s, b_s, g_vn, g_out_a, w_gf, b_gf, w_gb, b_gb, g_out_b, w_out, w_up, w_conv,
                   b_conv, w_down)
```

```python
import functools

import jax
import jax.numpy as jnp
from jax import lax
from jax.experimental import pallas as pl
from jax.experimental.pallas import tpu as pltpu

F32 = jnp.float32
BF16 = jnp.bfloat16

D_MODEL = 1024
N_MOD = 6
EPS = 1e-6
SG_HEADS = 4
SG_WIDTH = 512
SG_HEAD_DIM = 128
SG_CHUNK = 128
GLA_HEADS = 4
GLA_DK = 64
GLA_DV = 128
GLA_K_WIDTH = 256
GLA_V_WIDTH = 512
GLA_GATE_RANK = 16
GLA_GATE_NORMALIZER = 16.0
GLA_CHUNK = 128
COL_U, COL_VA, COL_Q, COL_V, COL_R, COL_LR, COL_END = 0, 512, 1024, 1536, 2048, 2560, 2592
LR_PAD = 128
D_FF = 2816
FF_BLOCK = 256

MIX_TILE = 512
VMEM_LIMIT = 56 * 1024 * 1024


def _sigmoid(x):
    return 1.0 / (1.0 + jnp.exp(-x))


def _log_sigmoid(x):
    return jnp.minimum(x, 0.0) - jnp.log(1.0 + jnp.exp(-jnp.abs(x)))


def _rms_scale(x):
    return x * lax.rsqrt(jnp.mean(x * x, axis=-1, keepdims=True) + EPS)


def _adaln(x, gain_row, scale_row, shift_row):
    return _rms_scale(x) * (gain_row * (1.0 + scale_row)) + shift_row


def _const_spec(shape):
    nd = len(shape)
    return pl.BlockSpec(shape, lambda *_: (0,) * nd, pipeline_mode=pl.Buffered(1))


def _mod_kernel(c_ref, w_ref, b_ref, o_ref):
    c = c_ref[...]
    s = (c * _sigmoid(c)).astype(BF16)
    o_ref[0] = jnp.dot(s, w_ref[0].astype(BF16), preferred_element_type=F32) + b_ref[0]


def _modulation(c_all, w_ada, b_ada):
    depth = w_ada.shape[0]
    rows = c_all.shape[0]
    cols = w_ada.shape[2]
    bn = 1536
    return pl.pallas_call(
        _mod_kernel,
        grid=(depth, cols // bn),
        in_specs=[
            pl.BlockSpec((rows, D_MODEL), lambda l, j: (0, 0)),
            pl.BlockSpec((1, D_MODEL, bn), lambda l, j: (l, 0, j)),
            pl.BlockSpec((1, 1, bn), lambda l, j: (l, 0, j)),
        ],
        out_specs=pl.BlockSpec((1, rows, bn), lambda l, j: (l, 0, j)),
        out_shape=jax.ShapeDtypeStruct((depth, rows, cols), F32),
        compiler_params=pltpu.CompilerParams(vmem_limit_bytes=VMEM_LIMIT),
        name="adaln_mod",
    )(c_all, w_ada, b_ada.reshape(depth, 1, cols))


def _gla_tile(qk_ref, v_ref, gate_ref, s_ref, write_o, reverse):
    tt = gate_ref.shape[0]
    c = GLA_CHUNK
    n_c = tt // c
    n_p = GLA_HEADS // 2
    kw = GLA_K_WIDTH
    pk, pv = 2 * GLA_DK, 2 * GLA_DV
    row = lax.broadcasted_iota(jnp.int32, (c, c), 0)
    col = lax.broadcasted_iota(jnp.int32, (c, c), 1)
    causal = (col >= row) if reverse else (col <= row)
    tri = jnp.where(causal, 1.0, 0.0).astype(BF16)
    causal2 = jnp.concatenate([causal, causal], axis=1)
    first_k = lax.broadcasted_iota(jnp.int32, (c, pk), 1) < GLA_DK
    first_v = lax.broadcasted_iota(jnp.int32, (c, pv), 1) < GLA_DV
    diag = (lax.broadcasted_iota(jnp.int32, (pk, pv), 0) < GLA_DK) == (lax.broadcasted_iota(jnp.int32, (pk, pv), 1) < GLA_DV)
    rep_row = lax.broadcasted_iota(jnp.int32, (16, kw), 0)
    ones16 = jnp.ones((16, pv), BF16)
    zero_k = jnp.zeros((c, pk), BF16)
    zero_v = jnp.zeros((c, pv), BF16)
    tn = (((0,), (0,)), ((), ()))
    nt = (((1,), (1,)), ((), ()))
    chunks = range(n_c)
    rows = [slice(ci * c, (ci + 1) * c) for ci in chunks]

    g = _log_sigmoid(gate_ref[...]) * (1.0 / GLA_GATE_NORMALIZER)
    g_hi = g.astype(BF16)
    g_lo = (g - g_hi.astype(F32)).astype(BF16)
    g_cat = jnp.concatenate([part[r] for r in rows for part in (g_hi, g_lo)], axis=1)
    bb = jnp.dot(tri, g_cat, preferred_element_type=F32)
    b = [bb[:, 2 * ci * kw:(2 * ci + 1) * kw] + bb[:, (2 * ci + 1) * kw:(2 * ci + 2) * kw] for ci in chunks]
    b_mid = [x[c // 2:c // 2 + 1] if reverse else x[c // 2 - 1:c // 2] for x in b]
    b_last = [x[0:1] if reverse else x[c - 1:c] for x in b]
    yield

    q_in, q_dec, k_dec, k_end, reps = [], [], [], [], []
    for ci in chunks:
        qk = qk_ref[rows[ci], :].astype(F32)
        q = qk[:, :kw] * (GLA_DK ** -0.5)
        k = qk[:, kw:]
        q_in.append((q * jnp.exp(b[ci])).astype(BF16))
        q_dec.append((q * jnp.exp(b[ci] - b_mid[ci])).astype(BF16))
        k_dec.append((k * jnp.exp(b_mid[ci] - b[ci])).astype(BF16))
        k_end.append((k * jnp.exp(b_last[ci] - b[ci])).astype(BF16))
        bl_hi = b_last[ci].astype(BF16).astype(F32)
        reps.append(jnp.where(rep_row == 0, bl_hi, jnp.where(rep_row == 1, b_last[ci] - bl_hi, 0.0)).astype(BF16))
        if ci % 2 == 1:
            yield

    pairs = [(ci, p) for ci in chunks for p in range(n_p)]
    ks = [slice(p * pk, (p + 1) * pk) for p in range(n_p)]
    vp = {(ci, p): v_ref[rows[ci], p * pv:(p + 1) * pv] for ci, p in pairs}
    scores, d_s, decay = {}, {}, {}
    for ci, p in pairs:
        kp = k_dec[ci][:, ks[p]]
        k_bd = jnp.concatenate([jnp.where(first_k, kp, zero_k), jnp.where(first_k, zero_k, kp)], axis=0)
        scores[ci, p] = lax.dot_general(q_dec[ci][:, ks[p]], k_bd, nt, preferred_element_type=F32)
    yield
    for ci, p in pairs:
        d_s[ci, p] = lax.dot_general(k_end[ci][:, ks[p]], vp[ci, p], tn, preferred_element_type=F32)
        decay[ci, p] = lax.dot_general(reps[ci][:, ks[p]], ones16, tn, preferred_element_type=F32)
    yield
    o = {}
    for ci, p in pairs:
        v_bd = jnp.concatenate([jnp.where(first_v, vp[ci, p], zero_v), jnp.where(first_v, zero_v, vp[ci, p])], axis=0)
        masked = jnp.where(causal2, scores[ci, p], 0.0).astype(BF16)
        o[ci, p] = jnp.dot(masked, v_bd, preferred_element_type=F32)
    yield
    states = {}
    for p in range(n_p):
        s = s_ref[p]
        for ci in (reversed(chunks) if reverse else chunks):
            states[ci, p] = s.astype(BF16)
            s = jnp.exp(decay[ci, p]) * s + jnp.where(diag, d_s[ci, p], 0.0)
        s_ref[p] = s
    for ci in chunks:
        write_o(rows[ci], jnp.concatenate(
            [o[ci, p] + jnp.dot(q_in[ci][:, ks[p]], states[ci, p], preferred_element_type=F32) for p in range(n_p)],
            axis=1))


def _mix_fwd_kernel(x_ref, mod_ref, gpre_ref, wmain_ref, wlr_ref, wg_ref, bg_ref, ws_ref, bs_ref, gvn_ref, goa_ref,
                    a_ref, qk_ref, v_ref, r_ref, lr_ref, of_ref,
                    s_ref, qkf_ref, gate_ref):
    tt = x_ref.shape[1]

    @pl.when(pl.program_id(1) == 0)
    def _():
        s_ref[...] = jnp.zeros_like(s_ref)

    mod = mod_ref[0]
    hb = _adaln(x_ref[0], gpre_ref[...], mod[1:2], mod[0:1]).astype(BF16)
    proj = jnp.dot(hb, wmain_ref[...], preferred_element_type=F32)
    lr = jnp.dot(hb, wlr_ref[...], preferred_element_type=F32).astype(BF16)
    lr_ref[0] = lr
    gate_ref[...] = jnp.dot(lr, wg_ref[:, :GLA_K_WIDTH], preferred_element_type=F32) + bg_ref[:, :GLA_K_WIDTH]
    qkf_ref[...] = proj[:, COL_Q:COL_V]
    qk_ref[0] = proj[:, COL_Q:COL_V].astype(BF16)
    v_ref[0] = proj[:, COL_V:COL_R].astype(BF16)
    r_ref[0] = proj[:, COL_R:COL_LR].astype(BF16)

    n_sg = tt // SG_CHUNK
    heads = [slice(hd * SG_HEAD_DIM, (hd + 1) * SG_HEAD_DIM) for hd in range(SG_HEADS)]
    vn_cat = []
    for cs in heads:
        vh = proj[:, COL_VA + cs.start:COL_VA + cs.stop]
        xc = vh - jnp.mean(vh, axis=-1, keepdims=True)
        vn = (xc * lax.rsqrt(jnp.mean(xc * xc, axis=-1, keepdims=True) + EPS) * gvn_ref[:, cs]).astype(BF16)
        vn_cat.append(jnp.concatenate([vn[i * SG_CHUNK:(i + 1) * SG_CHUNK] for i in range(n_sg)], axis=1))
    mixed_cat = [jnp.dot(ws_ref[hd], vn_cat[hd], preferred_element_type=F32) for hd in range(SG_HEADS)]
    a_heads = []
    for hd, cs in enumerate(heads):
        mixed = jnp.concatenate(
            [mixed_cat[hd][:, i * SG_CHUNK:(i + 1) * SG_CHUNK] + bs_ref[hd] for i in range(n_sg)], axis=0)
        ah = proj[:, COL_U + cs.start:COL_U + cs.stop] * mixed
        a_heads.append((_rms_scale(ah) * goa_ref[:, cs]).astype(BF16))
    a_ref[0] = jnp.concatenate(a_heads, axis=1)

    def write_o(rows, o):
        of_ref[0, rows, :] = o.astype(of_ref.dtype)

    for _ in _gla_tile(qkf_ref, v_ref.at[0], gate_ref, s_ref, write_o, False):
        pass


def _mix_fwd(x, mod, g_pre, w_main, w_lr, w_g, b_g, w_s, bs_full, g_vn, g_out_a):
    bsz, seq, _ = x.shape
    tt = min(MIX_TILE, seq)
    n_t = seq // tt
    tok = lambda w: pl.BlockSpec((1, tt, w), lambda b, t: (b, t, 0))
    out_w = (SG_WIDTH, 2 * GLA_K_WIDTH, GLA_V_WIDTH, GLA_V_WIDTH, LR_PAD, GLA_V_WIDTH)
    return pl.pallas_call(
        _mix_fwd_kernel,
        grid=(bsz, n_t),
        in_specs=[
            tok(D_MODEL),
            pl.BlockSpec((1, N_MOD, D_MODEL), lambda b, t: (b, 0, 0)),
            _const_spec((1, D_MODEL)),
            _const_spec((D_MODEL, COL_LR)),
            _const_spec((D_MODEL, LR_PAD)),
            _const_spec((LR_PAD, 2 * GLA_K_WIDTH)),
            _const_spec((1, 2 * GLA_K_WIDTH)),
            _const_spec((SG_HEADS, SG_CHUNK, SG_CHUNK)),
            _const_spec((SG_HEADS, SG_CHUNK, SG_HEAD_DIM)),
            _const_spec((1, SG_WIDTH)),
            _const_spec((1, SG_WIDTH)),
        ],
        out_specs=[tok(w) for w in out_w],
        out_shape=[jax.ShapeDtypeStruct((bsz, seq, w), BF16) for w in out_w],
        scratch_shapes=[
            pltpu.VMEM((GLA_HEADS // 2, 2 * GLA_DK, 2 * GLA_DV), F32),
            pltpu.VMEM((tt, 2 * GLA_K_WIDTH), F32),
            pltpu.VMEM((tt, GLA_K_WIDTH), F32),
        ],
        compiler_params=pltpu.CompilerParams(
            dimension_semantics=("arbitrary", "arbitrary"), vmem_limit_bytes=VMEM_LIMIT),
        name="mix_fwd",
    )(x, mod, g_pre, w_main, w_lr, w_g, b_g, w_s, bs_full, g_vn, g_out_a)


RING = 3
NEXT_RING = 4
SIDE_SLICES = 4


def _mixer_bwd_pieces(x_ref, mod_ref, a_ref, qk_ref, v_ref, r_ref, lr_ref, of_ref, wg_ref, bg_ref, gob_ref, wout_ref,
                      gpost_ref, s_ref, gate_ref, o_ref, store_mid):
    gate_ref[...] = jnp.dot(lr_ref[0], wg_ref[:, GLA_K_WIDTH:], preferred_element_type=F32) + bg_ref[:, GLA_K_WIDTH:]

    def write_o(rows, o):
        o_ref[rows, :] = o + of_ref[0, rows, :].astype(F32)

    yield from _gla_tile(qk_ref.at[0], v_ref.at[0], gate_ref, s_ref, write_o, True)
    yield
    r = r_ref[0].astype(F32)
    swish = r * _sigmoid(r)
    o_heads = []
    for hd in range(GLA_HEADS):
        vs = slice(hd * GLA_DV, (hd + 1) * GLA_DV)
        o_heads.append((_rms_scale(o_ref[:, vs]) * (gob_ref[:, vs] * swish[:, vs])).astype(BF16))
    cat = jnp.concatenate([a_ref[0]] + o_heads, axis=1)
    yield "before output projection"
    y = jnp.dot(cat, wout_ref[...], preferred_element_type=F32)
    yield
    mod = mod_ref[0]
    store_mid(x_ref[0] + _rms_scale(y) * (mod[2:3] * gpost_ref[...]))


def _ffn_pieces(tt, ahead_ref, done_ref, prev8, next8, has_prev, has_next, hb_ref, hb_ahead_ref, out_ref,
                mod_ref, mod_ahead_ref, mod_done_ref, gpre_ref, wup_ref, wconv_ref, bconv_ref, wdown_ref, gpost_ref,
                y_ref, act_ref):
    pitch = tt // 8
    mod = mod_ref[0]
    mod_a = mod_ahead_ref[0]
    mod_d = mod_done_ref[0]
    sub = lax.broadcasted_iota(jnp.int32, (8, 1), 0)
    halo = jnp.where(sub == 0, pltpu.roll(prev8, 1, 0), pltpu.roll(next8, 7, 0))
    keep = jnp.logical_or(jnp.logical_and(sub == 0, has_prev), jnp.logical_and(sub == 7, has_next))
    hh = jnp.where(keep, _adaln(halo, gpre_ref[...], mod[4:5], mod[3:4]), 0.0)
    hb_ref[tt:tt + 16, :] = jnp.concatenate([hh, jnp.zeros_like(hh)], axis=0).astype(BF16)
    yield
    n_blocks = D_FF // FF_BLOCK
    w = pitch // SIDE_SLICES
    for j in range(n_blocks):
        conv = []
        for part in range(2):
            c0 = part * D_FF + j * FF_BLOCK
            cols = slice(c0, c0 + FF_BLOCK)
            u = jnp.dot(hb_ref[...], wup_ref[:, cols], preferred_element_type=F32)
            edge = u[tt:tt + 8]
            before = jnp.concatenate([jnp.where(sub == 0, edge, pltpu.roll(u[tt - 8:tt], 1, 0)), u[0:tt - 8]], axis=0)
            after = jnp.concatenate([u[8:tt], jnp.where(sub == 7, edge, pltpu.roll(u[0:8], 7, 0))], axis=0)
            conv.append(before * wconv_ref[0:1, cols] + u[0:tt] * wconv_ref[1:2, cols] + after * wconv_ref[2:3, cols]
                        + bconv_ref[:, cols])
        gate, val = conv
        act_ref[:, j * FF_BLOCK:(j + 1) * FF_BLOCK] = (gate * _sigmoid(gate) * val).astype(BF16)
        if j < SIDE_SLICES:
            i = j
            upd = _rms_scale(out_ref[i * 8 * w:(i + 1) * 8 * w, :]) * (mod_d[5:6] * gpost_ref[...])
            upd = pltpu.einshape("msd->smd", upd.reshape(w, 8, D_MODEL))
            for s in range(8):
                rows = slice(s * pitch + i * w, s * pitch + (i + 1) * w)
                y_ref[0, rows, :] = done_ref[rows, :] + upd[s]
        i = j - (n_blocks - SIDE_SLICES)
        if i >= 0:
            x3 = jnp.stack([ahead_ref[s * pitch + i * w:s * pitch + (i + 1) * w, :] for s in range(8)], axis=0)
            xq = pltpu.einshape("smd->msd", x3).reshape(8 * w, D_MODEL)
            hb_ahead_ref[i * 8 * w:(i + 1) * 8 * w, :] = _adaln(xq, gpre_ref[...], mod_a[4:5], mod_a[3:4]).astype(BF16)
        yield
    hb_ref[0:tt, :] = hb_ahead_ref[...]
    yield "before down projection"
    out_ref[...] = jnp.dot(act_ref[...], wdown_ref[...], preferred_element_type=F32)


def _tail_kernel(x_ref, modm_ref, a_ref, qk_ref, v_ref, r_ref, lr_ref, of_ref, modf_ref, moda_ref, modd_ref,
                 wg_ref, bg_ref, gob_ref, wout_ref, gpost_ref,
                 gpre_ref, wup_ref, wconv_ref, bconv_ref, wdown_ref, gpostf_ref,
                 y_ref,
                 s_ref, gate_ref, o_ref, ring_ref, next_ref, hb_ref, hb_ahead_ref, out_ref, act_ref, *, n_t, n_real):
    g = pl.program_id(0)
    tt = x_ref.shape[1]

    @pl.when(g == 0)
    def _():
        ring_ref[...] = jnp.zeros_like(ring_ref)
        next_ref[...] = jnp.zeros_like(next_ref)
        hb_ref[...] = jnp.zeros_like(hb_ref)
        hb_ahead_ref[...] = jnp.zeros_like(hb_ahead_ref)
        out_ref[...] = jnp.zeros_like(out_ref)

    @pl.when(g % n_t == 0)
    def _():
        s_ref[...] = jnp.zeros_like(s_ref)

    f = jnp.clip(g - 2, 0, n_real - 1)
    tile_f = n_t - 1 - f % n_t
    ffn = _ffn_pieces(tt, ring_ref.at[(g + 2) % RING], ring_ref.at[g % RING],
                      ring_ref[(g + 2) % RING, tt - 8:tt, :], next_ref[(g + 1) % NEXT_RING],
                      tile_f > 0, tile_f < n_t - 1, hb_ref, hb_ahead_ref, out_ref, modf_ref, moda_ref, modd_ref,
                      gpre_ref, wup_ref, wconv_ref, bconv_ref, wdown_ref, gpostf_ref, y_ref, act_ref)

    def store_mid(y):
        ring_ref[g % RING] = y
        next_ref[g % NEXT_RING] = y[0:8]

    mixer = _mixer_bwd_pieces(x_ref, modm_ref, a_ref, qk_ref, v_ref, r_ref, lr_ref, of_ref, wg_ref, bg_ref, gob_ref,
                              wout_ref, gpost_ref, s_ref, gate_ref, o_ref, store_mid)

    next(ffn)
    tag = None
    while tag != "before output projection":
        next(ffn)
        tag = next(mixer)
    for tag in ffn:
        if tag == "before down projection":
            break
    next(mixer)
    for _ in ffn:
        pass
    for _ in mixer:
        pass


def _tail(x, mod, a, qk, v, r, lr, o_fwd, w_g, b_g, g_out_b, w_out, g_post, g_pre_f, w_up, w_conv, b_conv, w_down,
          g_post_f):
    bsz, seq, _ = x.shape
    tt = min(MIX_TILE, seq)
    n_t = seq // tt
    n_real = bsz * n_t

    def pos(g, lag):
        m = jnp.clip(g - lag, 0, n_real - 1)
        return m // n_t, n_t - 1 - m % n_t

    tok = lambda w: pl.BlockSpec((1, tt, w), lambda g: (*pos(g, 0), 0))
    mod_spec = lambda lag: pl.BlockSpec((1, N_MOD, D_MODEL), lambda g: (pos(g, lag)[0], 0, 0))
    return pl.pallas_call(
        functools.partial(_tail_kernel, n_t=n_t, n_real=n_real),
        grid=(n_real + 3,),
        in_specs=[
            tok(D_MODEL),
            mod_spec(0),
            tok(SG_WIDTH), tok(2 * GLA_K_WIDTH), tok(GLA_V_WIDTH), tok(GLA_V_WIDTH), tok(LR_PAD), tok(GLA_V_WIDTH),
            mod_spec(2),
            mod_spec(1),
            mod_spec(3),
            _const_spec((LR_PAD, 2 * GLA_K_WIDTH)),
            _const_spec((1, 2 * GLA_K_WIDTH)),
            _const_spec((1, GLA_V_WIDTH)),
            _const_spec((D_MODEL, D_MODEL)),
            _const_spec((1, D_MODEL)),
            _const_spec((1, D_MODEL)),
            _const_spec((D_MODEL, 2 * D_FF)),
            _const_spec((3, 2 * D_FF)),
            _const_spec((1, 2 * D_FF)),
            _const_spec((D_FF, D_MODEL)),
            _const_spec((1, D_MODEL)),
        ],
        out_specs=pl.BlockSpec((1, tt, D_MODEL), lambda g: (*pos(g, 3), 0)),
        out_shape=jax.ShapeDtypeStruct((bsz, seq, D_MODEL), F32),
        scratch_shapes=[
            pltpu.VMEM((GLA_HEADS // 2, 2 * GLA_DK, 2 * GLA_DV), F32),
            pltpu.VMEM((tt, GLA_K_WIDTH), F32),
            pltpu.VMEM((tt, GLA_V_WIDTH), F32),
            pltpu.VMEM((RING, tt, D_MODEL), F32),
            pltpu.VMEM((NEXT_RING, 8, D_MODEL), F32),
            pltpu.VMEM((tt + 16, D_MODEL), BF16),
            pltpu.VMEM((tt, D_MODEL), BF16),
            pltpu.VMEM((tt, D_MODEL), F32),
            pltpu.VMEM((tt, D_FF), BF16),
        ],
        compiler_params=pltpu.CompilerParams(dimension_semantics=("arbitrary",), vmem_limit_bytes=VMEM_LIMIT),
        name="tail",
    )(x, mod, a, qk, v, r, lr, o_fwd, mod, mod, mod, w_g, b_g, g_out_b, w_out, g_post, g_pre_f, w_up, w_conv, b_conv,
      w_down, g_post_f)


def _layer_params(l, g_pre_mix, g_post_mix, g_pre_ffn, g_post_ffn, w_in, w_s, b_s, g_vn, g_out_a, w_gf, b_gf, w_gb,
                  b_gb, g_out_b, w_out, w_up, w_conv, b_conv, w_down):
    row = lambda a: a[l].reshape(1, -1)
    w_lr = jnp.pad(w_in[l][:, COL_LR:COL_END], ((0, 0), (0, LR_PAD - (COL_END - COL_LR)))).astype(BF16)
    w_g = jnp.zeros((LR_PAD, 2 * GLA_K_WIDTH), F32)
    w_g = w_g.at[:GLA_GATE_RANK, :GLA_K_WIDTH].set(w_gf[l])
    w_g = w_g.at[GLA_GATE_RANK:2 * GLA_GATE_RANK, GLA_K_WIDTH:].set(w_gb[l]).astype(BF16)
    b_g = jnp.concatenate([b_gf[l], b_gb[l]]).reshape(1, -1)
    bs_full = jnp.broadcast_to(b_s[l][:, :, None], (SG_HEADS, SG_CHUNK, SG_HEAD_DIM))
    return dict(
        g_pre_mix=row(g_pre_mix), g_post_mix=row(g_post_mix), g_pre_ffn=row(g_pre_ffn), g_post_ffn=row(g_post_ffn),
        w_main=w_in[l][:, :COL_LR].astype(BF16), w_lr=w_lr, w_g=w_g, b_g=b_g,
        w_s=w_s[l].astype(BF16), bs_full=bs_full, g_vn=row(g_vn), g_out_a=row(g_out_a), g_out_b=row(g_out_b),
        w_out=w_out[l].astype(BF16), w_up=w_up[l].astype(BF16), w_conv=w_conv[l], b_conv=row(b_conv),
        w_down=w_down[l].astype(BF16))


def _layer(x, mod, p):
    a, qk, v, r, lr, o_fwd = _mix_fwd(x, mod, p["g_pre_mix"], p["w_main"], p["w_lr"], p["w_g"], p["b_g"], p["w_s"],
                                      p["bs_full"], p["g_vn"], p["g_out_a"])
    return _tail(x, mod, a, qk, v, r, lr, o_fwd, p["w_g"], p["b_g"], p["g_out_b"], p["w_out"], p["g_post_mix"],
                 p["g_pre_ffn"], p["w_up"], p["w_conv"], p["b_conv"], p["w_down"], p["g_post_ffn"])


def _trunks(xs, cs, w_ada, b_ada, *weights):
    depth = w_ada.shape[0]
    sizes = [c.shape[0] for c in cs]
    mod_all = _modulation(jnp.concatenate(cs, axis=0), w_ada, b_ada)
    xs = list(xs)
    for l in range(depth):
        p = _layer_params(l, *weights)
        start = 0
        for i, n in enumerate(sizes):
            mod = mod_all[l, start:start + n].reshape(n, N_MOD, D_MODEL)
            xs[i] = _layer(xs[i], mod, p)
            start += n
    return tuple(xs)


def kernel(x_prompt, x_sample, c_prompt, c_sample, w_ada, b_ada, g_pre_mix, g_post_mix, g_pre_ffn, g_post_ffn, w_in, w_s, b_s, g_vn, g_out_a, w_gf, b_gf, w_gb, b_gb, g_out_b, w_out, w_up, w_conv, b_conv, w_down):
    return _trunks((x_prompt, x_sample), (c_prompt, c_sample), w_ada, b_ada, g_pre_mix, g_post_mix, g_pre_ffn,
                   g_post_ffn, w_in, w_s, b_s, g_vn, g_out_a, w_gf, b_gf, w_gb, b_gb, g_out_b, w_out, w_up, w_conv,
                   b_conv, w_down)
```

```python
import functools

import jax
import jax.numpy as jnp
from jax import lax
from jax.experimental import pallas as pl
from jax.experimental.pallas import tpu as pltpu

F32 = jnp.float32
BF16 = jnp.bfloat16

D_MODEL = 1024
N_MOD = 6
EPS = 1e-6
SG_HEADS = 4
SG_WIDTH = 512
SG_HEAD_DIM = 128
SG_CHUNK = 128
GLA_HEADS = 4
GLA_DK = 64
GLA_DV = 128
GLA_K_WIDTH = 256
GLA_V_WIDTH = 512
GLA_GATE_RANK = 16
GLA_GATE_NORMALIZER = 16.0
GLA_CHUNK = 128
COL_U, COL_VA, COL_Q, COL_V, COL_R, COL_LR, COL_END = 0, 512, 1024, 1536, 2048, 2560, 2592
LR_PAD = 128
D_FF = 2816
FF_BLOCK = 256
CONV_HALO = 8

MIX_TILE = 1024
FFN_TILE = 1024
VMEM_LIMIT = 56 * 1024 * 1024


def _sigmoid(x):
    return 1.0 / (1.0 + jnp.exp(-x))


def _log_sigmoid(x):
    return jnp.minimum(x, 0.0) - jnp.log(1.0 + jnp.exp(-jnp.abs(x)))


def _rms_scale(x):
    return x * lax.rsqrt(jnp.mean(x * x, axis=-1, keepdims=True) + EPS)


def _adaln(x, gain_row, scale_row, shift_row):
    return _rms_scale(x) * (gain_row * (1.0 + scale_row)) + shift_row


def _const_spec(shape):
    nd = len(shape)
    return pl.BlockSpec(shape, lambda *_: (0,) * nd, pipeline_mode=pl.Buffered(1))


def _mod_kernel(c_ref, w_ref, b_ref, o_ref):
    c = c_ref[...]
    s = (c * _sigmoid(c)).astype(BF16)
    o_ref[0] = jnp.dot(s, w_ref[0].astype(BF16), preferred_element_type=F32) + b_ref[0]


def _modulation(c_all, w_ada, b_ada):
    depth = w_ada.shape[0]
    rows = c_all.shape[0]
    cols = w_ada.shape[2]
    bn = 1536
    return pl.pallas_call(
        _mod_kernel,
        grid=(depth, cols // bn),
        in_specs=[
            pl.BlockSpec((rows, D_MODEL), lambda l, j: (0, 0)),
            pl.BlockSpec((1, D_MODEL, bn), lambda l, j: (l, 0, j)),
            pl.BlockSpec((1, 1, bn), lambda l, j: (l, 0, j)),
        ],
        out_specs=pl.BlockSpec((1, rows, bn), lambda l, j: (l, 0, j)),
        out_shape=jax.ShapeDtypeStruct((depth, rows, cols), F32),
        compiler_params=pltpu.CompilerParams(vmem_limit_bytes=VMEM_LIMIT),
        name="adaln_mod",
    )(c_all, w_ada, b_ada.reshape(depth, 1, cols))


def _gla_tile(qk_ref, v_ref, gate_ref, s_ref, write_o, reverse):
    tt = gate_ref.shape[0]
    c = GLA_CHUNK
    n_c = tt // c
    n_p = GLA_HEADS // 2
    kw = GLA_K_WIDTH
    pk, pv = 2 * GLA_DK, 2 * GLA_DV
    row = lax.broadcasted_iota(jnp.int32, (c, c), 0)
    col = lax.broadcasted_iota(jnp.int32, (c, c), 1)
    causal = (col >= row) if reverse else (col <= row)
    tri = jnp.where(causal, 1.0, 0.0).astype(BF16)
    causal2 = jnp.concatenate([causal, causal], axis=1)
    first_k = lax.broadcasted_iota(jnp.int32, (c, pk), 1) < GLA_DK
    first_v = lax.broadcasted_iota(jnp.int32, (c, pv), 1) < GLA_DV
    diag = (lax.broadcasted_iota(jnp.int32, (pk, pv), 0) < GLA_DK) == (lax.broadcasted_iota(jnp.int32, (pk, pv), 1) < GLA_DV)
    rep_row = lax.broadcasted_iota(jnp.int32, (16, kw), 0)
    ones16 = jnp.ones((16, pv), BF16)
    zero_k = jnp.zeros((c, pk), BF16)
    zero_v = jnp.zeros((c, pv), BF16)
    tn = (((0,), (0,)), ((), ()))
    nt = (((1,), (1,)), ((), ()))
    chunks = range(n_c)
    rows = [slice(ci * c, (ci + 1) * c) for ci in chunks]

    g = _log_sigmoid(gate_ref[...]) * (1.0 / GLA_GATE_NORMALIZER)
    g_hi = g.astype(BF16)
    g_lo = (g - g_hi.astype(F32)).astype(BF16)
    g_cat = jnp.concatenate([part[r] for r in rows for part in (g_hi, g_lo)], axis=1)
    bb = jnp.dot(tri, g_cat, preferred_element_type=F32)
    b = [bb[:, 2 * ci * kw:(2 * ci + 1) * kw] + bb[:, (2 * ci + 1) * kw:(2 * ci + 2) * kw] for ci in chunks]
    b_mid = [x[c // 2:c // 2 + 1] if reverse else x[c // 2 - 1:c // 2] for x in b]
    b_last = [x[0:1] if reverse else x[c - 1:c] for x in b]

    q_in, q_dec, k_dec, k_end, reps = [], [], [], [], []
    for ci in chunks:
        qk = qk_ref[rows[ci], :].astype(F32)
        q = qk[:, :kw] * (GLA_DK ** -0.5)
        k = qk[:, kw:]
        q_in.append((q * jnp.exp(b[ci])).astype(BF16))
        q_dec.append((q * jnp.exp(b[ci] - b_mid[ci])).astype(BF16))
        k_dec.append((k * jnp.exp(b_mid[ci] - b[ci])).astype(BF16))
        k_end.append((k * jnp.exp(b_last[ci] - b[ci])).astype(BF16))
        bl_hi = b_last[ci].astype(BF16).astype(F32)
        reps.append(jnp.where(rep_row == 0, bl_hi, jnp.where(rep_row == 1, b_last[ci] - bl_hi, 0.0)).astype(BF16))

    pairs = [(ci, p) for ci in chunks for p in range(n_p)]
    ks = [slice(p * pk, (p + 1) * pk) for p in range(n_p)]
    vp = {(ci, p): v_ref[rows[ci], p * pv:(p + 1) * pv] for ci, p in pairs}
    scores, d_s, decay = {}, {}, {}
    for ci, p in pairs:
        kp = k_dec[ci][:, ks[p]]
        k_bd = jnp.concatenate([jnp.where(first_k, kp, zero_k), jnp.where(first_k, zero_k, kp)], axis=0)
        scores[ci, p] = lax.dot_general(q_dec[ci][:, ks[p]], k_bd, nt, preferred_element_type=F32)
    for ci, p in pairs:
        d_s[ci, p] = lax.dot_general(k_end[ci][:, ks[p]], vp[ci, p], tn, preferred_element_type=F32)
        decay[ci, p] = lax.dot_general(reps[ci][:, ks[p]], ones16, tn, preferred_element_type=F32)
    o = {}
    for ci, p in pairs:
        v_bd = jnp.concatenate([jnp.where(first_v, vp[ci, p], zero_v), jnp.where(first_v, zero_v, vp[ci, p])], axis=0)
        masked = jnp.where(causal2, scores[ci, p], 0.0).astype(BF16)
        o[ci, p] = jnp.dot(masked, v_bd, preferred_element_type=F32)
    states = {}
    for p in range(n_p):
        s = s_ref[p]
        for ci in (reversed(chunks) if reverse else chunks):
            states[ci, p] = s.astype(BF16)
            s = jnp.exp(decay[ci, p]) * s + jnp.where(diag, d_s[ci, p], 0.0)
        s_ref[p] = s
    for ci in chunks:
        write_o(rows[ci], jnp.concatenate(
            [o[ci, p] + jnp.dot(q_in[ci][:, ks[p]], states[ci, p], preferred_element_type=F32) for p in range(n_p)],
            axis=1))


def _mix_fwd_kernel(x_ref, mod_ref, gpre_ref, wmain_ref, wlr_ref, wg_ref, bg_ref, ws_ref, bs_ref, gvn_ref, goa_ref,
                    a_ref, qk_ref, v_ref, r_ref, lr_ref, of_ref,
                    s_ref, qkf_ref, gate_ref):
    tt = x_ref.shape[1]

    @pl.when(pl.program_id(1) == 0)
    def _():
        s_ref[...] = jnp.zeros_like(s_ref)

    mod = mod_ref[0]
    hb = _adaln(x_ref[0], gpre_ref[...], mod[1:2], mod[0:1]).astype(BF16)
    proj = jnp.dot(hb, wmain_ref[...], preferred_element_type=F32)
    lr = jnp.dot(hb, wlr_ref[...], preferred_element_type=F32).astype(BF16)
    lr_ref[0] = lr
    gate_ref[...] = jnp.dot(lr, wg_ref[:, :GLA_K_WIDTH], preferred_element_type=F32) + bg_ref[:, :GLA_K_WIDTH]
    qkf_ref[...] = proj[:, COL_Q:COL_V]
    qk_ref[0] = proj[:, COL_Q:COL_V].astype(BF16)
    v_ref[0] = proj[:, COL_V:COL_R].astype(BF16)
    r_ref[0] = proj[:, COL_R:COL_LR].astype(BF16)

    n_sg = tt // SG_CHUNK
    heads = [slice(hd * SG_HEAD_DIM, (hd + 1) * SG_HEAD_DIM) for hd in range(SG_HEADS)]
    vn_cat = []
    for cs in heads:
        vh = proj[:, COL_VA + cs.start:COL_VA + cs.stop]
        xc = vh - jnp.mean(vh, axis=-1, keepdims=True)
        vn = (xc * lax.rsqrt(jnp.mean(xc * xc, axis=-1, keepdims=True) + EPS) * gvn_ref[:, cs]).astype(BF16)
        vn_cat.append(jnp.concatenate([vn[i * SG_CHUNK:(i + 1) * SG_CHUNK] for i in range(n_sg)], axis=1))
    mixed_cat = [jnp.dot(ws_ref[hd], vn_cat[hd], preferred_element_type=F32) for hd in range(SG_HEADS)]
    a_heads = []
    for hd, cs in enumerate(heads):
        mixed = jnp.concatenate(
            [mixed_cat[hd][:, i * SG_CHUNK:(i + 1) * SG_CHUNK] + bs_ref[hd] for i in range(n_sg)], axis=0)
        ah = proj[:, COL_U + cs.start:COL_U + cs.stop] * mixed
        a_heads.append((_rms_scale(ah) * goa_ref[:, cs]).astype(BF16))
    a_ref[0] = jnp.concatenate(a_heads, axis=1)

    def write_o(rows, o):
        of_ref[0, rows, :] = o.astype(of_ref.dtype)

    _gla_tile(qkf_ref, v_ref.at[0], gate_ref, s_ref, write_o, False)


def _mix_fwd(x, mod, g_pre, w_main, w_lr, w_g, b_g, w_s, bs_full, g_vn, g_out_a):
    bsz, seq, _ = x.shape
    tt = min(MIX_TILE, seq)
    n_t = seq // tt
    tok = lambda w: pl.BlockSpec((1, tt, w), lambda b, t: (b, t, 0))
    out_w = (SG_WIDTH, 2 * GLA_K_WIDTH, GLA_V_WIDTH, GLA_V_WIDTH, LR_PAD, GLA_V_WIDTH)
    return pl.pallas_call(
        _mix_fwd_kernel,
        grid=(bsz, n_t),
        in_specs=[
            tok(D_MODEL),
            pl.BlockSpec((1, N_MOD, D_MODEL), lambda b, t: (b, 0, 0)),
            _const_spec((1, D_MODEL)),
            _const_spec((D_MODEL, COL_LR)),
            _const_spec((D_MODEL, LR_PAD)),
            _const_spec((LR_PAD, 2 * GLA_K_WIDTH)),
            _const_spec((1, 2 * GLA_K_WIDTH)),
            _const_spec((SG_HEADS, SG_CHUNK, SG_CHUNK)),
            _const_spec((SG_HEADS, SG_CHUNK, SG_HEAD_DIM)),
            _const_spec((1, SG_WIDTH)),
            _const_spec((1, SG_WIDTH)),
        ],
        out_specs=[tok(w) for w in out_w],
        out_shape=[jax.ShapeDtypeStruct((bsz, seq, w), BF16) for w in out_w],
        scratch_shapes=[
            pltpu.VMEM((GLA_HEADS // 2, 2 * GLA_DK, 2 * GLA_DV), F32),
            pltpu.VMEM((tt, 2 * GLA_K_WIDTH), F32),
            pltpu.VMEM((tt, GLA_K_WIDTH), F32),
        ],
        compiler_params=pltpu.CompilerParams(
            dimension_semantics=("arbitrary", "arbitrary"), vmem_limit_bytes=VMEM_LIMIT),
        name="mix_fwd",
    )(x, mod, g_pre, w_main, w_lr, w_g, b_g, w_s, bs_full, g_vn, g_out_a)


def _mix_bwd_kernel(x_ref, mod_ref, a_ref, qk_ref, v_ref, r_ref, lr_ref, of_ref, wg_ref, bg_ref, gob_ref, wout_ref,
                    gpost_ref, y_ref, s_ref, gate_ref, o_ref):
    tt = x_ref.shape[1]

    @pl.when(pl.program_id(1) == 0)
    def _():
        s_ref[...] = jnp.zeros_like(s_ref)

    gate_ref[...] = jnp.dot(lr_ref[0], wg_ref[:, GLA_K_WIDTH:], preferred_element_type=F32) + bg_ref[:, GLA_K_WIDTH:]

    def write_o(rows, o):
        o_ref[rows, :] = o + of_ref[0, rows, :].astype(F32)

    _gla_tile(qk_ref.at[0], v_ref.at[0], gate_ref, s_ref, write_o, True)

    r = r_ref[0].astype(F32)
    swish = r * _sigmoid(r)
    o_heads = []
    for hd in range(GLA_HEADS):
        vs = slice(hd * GLA_DV, (hd + 1) * GLA_DV)
        o_heads.append((_rms_scale(o_ref[:, vs]) * (gob_ref[:, vs] * swish[:, vs])).astype(BF16))
    cat = jnp.concatenate([a_ref[0]] + o_heads, axis=1)
    y = jnp.dot(cat, wout_ref[...], preferred_element_type=F32)
    mod = mod_ref[0]
    y_ref[0] = x_ref[0] + _rms_scale(y) * (mod[2:3] * gpost_ref[...])


def _mix_bwd(x, mod, a, qk, v, r, lr, o_fwd, w_g, b_g, g_out_b, w_out, g_post):
    bsz, seq, _ = x.shape
    tt = min(MIX_TILE, seq)
    n_t = seq // tt
    tok = lambda w: pl.BlockSpec((1, tt, w), lambda b, t: (b, n_t - 1 - t, 0))
    return pl.pallas_call(
        _mix_bwd_kernel,
        grid=(bsz, n_t),
        in_specs=[
            tok(D_MODEL),
            pl.BlockSpec((1, N_MOD, D_MODEL), lambda b, t: (b, 0, 0)),
            tok(SG_WIDTH), tok(2 * GLA_K_WIDTH), tok(GLA_V_WIDTH), tok(GLA_V_WIDTH), tok(LR_PAD), tok(GLA_V_WIDTH),
            _const_spec((LR_PAD, 2 * GLA_K_WIDTH)),
            _const_spec((1, 2 * GLA_K_WIDTH)),
            _const_spec((1, GLA_V_WIDTH)),
            _const_spec((D_MODEL, D_MODEL)),
            _const_spec((1, D_MODEL)),
        ],
        out_specs=tok(D_MODEL),
        out_shape=jax.ShapeDtypeStruct((bsz, seq, D_MODEL), F32),
        scratch_shapes=[
            pltpu.VMEM((GLA_HEADS // 2, 2 * GLA_DK, 2 * GLA_DV), F32),
            pltpu.VMEM((tt, GLA_K_WIDTH), F32),
            pltpu.VMEM((tt, GLA_V_WIDTH), F32),
        ],
        compiler_params=pltpu.CompilerParams(
            dimension_semantics=("arbitrary", "arbitrary"), vmem_limit_bytes=VMEM_LIMIT),
        name="mix_bwd",
    )(x, mod, a, qk, v, r, lr, o_fwd, w_g, b_g, g_out_b, w_out, g_post)


def _ffn_kernel(x_ref, xp_ref, xn_ref, mod_ref, gpre_ref, wup_ref, wconv_ref, bconv_ref, wdown_ref, gpost_ref,
                y_ref, act_ref):
    tt = x_ref.shape[1]
    pitch = tt // 8
    t = pl.program_id(1)
    n_t = pl.num_programs(1)
    mod = mod_ref[0]
    xq = pltpu.einshape("smd->msd", x_ref[0].reshape(8, pitch, D_MODEL)).reshape(tt, D_MODEL)
    sub = lax.broadcasted_iota(jnp.int32, (8, 1), 0)
    halo = jnp.where(sub == 0, pltpu.roll(xp_ref[0], 1, 0), pltpu.roll(xn_ref[0], 7, 0))
    keep = jnp.logical_or(jnp.logical_and(sub == 0, t > 0), jnp.logical_and(sub == 7, t < n_t - 1))
    hq = _adaln(xq, gpre_ref[...], mod[4:5], mod[3:4]).astype(BF16)
    hh = jnp.where(keep, _adaln(halo, gpre_ref[...], mod[4:5], mod[3:4]), 0.0)
    hb = jnp.concatenate([hq, jnp.concatenate([hh, jnp.zeros_like(hh)], axis=0).astype(BF16)], axis=0)
    for j in range(D_FF // FF_BLOCK):
        conv = []
        for part in range(2):
            c0 = part * D_FF + j * FF_BLOCK
            cols = slice(c0, c0 + FF_BLOCK)
            u = jnp.dot(hb, wup_ref[:, cols], preferred_element_type=F32)
            edge = u[tt:tt + 8]
            before = jnp.concatenate([jnp.where(sub == 0, edge, pltpu.roll(u[tt - 8:tt], 1, 0)), u[0:tt - 8]], axis=0)
            after = jnp.concatenate([u[8:tt], jnp.where(sub == 7, edge, pltpu.roll(u[0:8], 7, 0))], axis=0)
            conv.append(before * wconv_ref[0:1, cols] + u[0:tt] * wconv_ref[1:2, cols] + after * wconv_ref[2:3, cols]
                        + bconv_ref[:, cols])
        gate, val = conv
        act_ref[:, j * FF_BLOCK:(j + 1) * FF_BLOCK] = (gate * _sigmoid(gate) * val).astype(BF16)
    out = jnp.dot(act_ref[...], wdown_ref[...], preferred_element_type=F32)
    yq = xq + _rms_scale(out) * (mod[5:6] * gpost_ref[...])
    y_ref[0] = pltpu.einshape("msd->smd", yq.reshape(pitch, 8, D_MODEL)).reshape(tt, D_MODEL)


def _ffn(x, mod, g_pre, w_up, w_conv, b_conv, w_down, g_post):
    bsz, seq, _ = x.shape
    tt = min(FFN_TILE, seq)
    n_t = seq // tt
    per = tt // CONV_HALO
    n_halo = seq // CONV_HALO
    return pl.pallas_call(
        _ffn_kernel,
        grid=(bsz, n_t),
        in_specs=[
            pl.BlockSpec((1, tt, D_MODEL), lambda b, t: (b, t, 0)),
            pl.BlockSpec((1, CONV_HALO, D_MODEL), lambda b, t: (b, jnp.maximum(t * per - 1, 0), 0)),
            pl.BlockSpec((1, CONV_HALO, D_MODEL), lambda b, t: (b, jnp.minimum((t + 1) * per, n_halo - 1), 0)),
            pl.BlockSpec((1, N_MOD, D_MODEL), lambda b, t: (b, 0, 0)),
            _const_spec((1, D_MODEL)),
            _const_spec((D_MODEL, 2 * D_FF)),
            _const_spec((3, 2 * D_FF)),
            _const_spec((1, 2 * D_FF)),
            _const_spec((D_FF, D_MODEL)),
            _const_spec((1, D_MODEL)),
        ],
        out_specs=pl.BlockSpec((1, tt, D_MODEL), lambda b, t: (b, t, 0)),
        out_shape=jax.ShapeDtypeStruct((bsz, seq, D_MODEL), F32),
        scratch_shapes=[pltpu.VMEM((tt, D_FF), BF16)],
        compiler_params=pltpu.CompilerParams(
            dimension_semantics=("arbitrary", "arbitrary"), vmem_limit_bytes=VMEM_LIMIT),
        name="ffn",
    )(x, x, x, mod, g_pre, w_up, w_conv, b_conv, w_down, g_post)


def _layer_params(l, g_pre_mix, g_post_mix, g_pre_ffn, g_post_ffn, w_in, w_s, b_s, g_vn, g_out_a, w_gf, b_gf, w_gb,
                  b_gb, g_out_b, w_out, w_up, w_conv, b_conv, w_down):
    row = lambda a: a[l].reshape(1, -1)
    w_lr = jnp.pad(w_in[l][:, COL_LR:COL_END], ((0, 0), (0, LR_PAD - (COL_END - COL_LR)))).astype(BF16)
    w_g = jnp.zeros((LR_PAD, 2 * GLA_K_WIDTH), F32)
    w_g = w_g.at[:GLA_GATE_RANK, :GLA_K_WIDTH].set(w_gf[l])
    w_g = w_g.at[GLA_GATE_RANK:2 * GLA_GATE_RANK, GLA_K_WIDTH:].set(w_gb[l]).astype(BF16)
    b_g = jnp.concatenate([b_gf[l], b_gb[l]]).reshape(1, -1)
    bs_full = jnp.broadcast_to(b_s[l][:, :, None], (SG_HEADS, SG_CHUNK, SG_HEAD_DIM))
    return dict(
        g_pre_mix=row(g_pre_mix), g_post_mix=row(g_post_mix), g_pre_ffn=row(g_pre_ffn), g_post_ffn=row(g_post_ffn),
        w_main=w_in[l][:, :COL_LR].astype(BF16), w_lr=w_lr, w_g=w_g, b_g=b_g,
        w_s=w_s[l].astype(BF16), bs_full=bs_full, g_vn=row(g_vn), g_out_a=row(g_out_a), g_out_b=row(g_out_b),
        w_out=w_out[l].astype(BF16), w_up=w_up[l].astype(BF16), w_conv=w_conv[l], b_conv=row(b_conv),
        w_down=w_down[l].astype(BF16))


def _layer(x, mod, p):
    a, qk, v, r, lr, o_fwd = _mix_fwd(x, mod, p["g_pre_mix"], p["w_main"], p["w_lr"], p["w_g"], p["b_g"], p["w_s"],
                                      p["bs_full"], p["g_vn"], p["g_out_a"])
    x = _mix_bwd(x, mod, a, qk, v, r, lr, o_fwd, p["w_g"], p["b_g"], p["g_out_b"], p["w_out"], p["g_post_mix"])
    return _ffn(x, mod, p["g_pre_ffn"], p["w_up"], p["w_conv"], p["b_conv"], p["w_down"], p["g_post_ffn"])


def _trunks(xs, cs, w_ada, b_ada, *weights):
    depth = w_ada.shape[0]
    sizes = [c.shape[0] for c in cs]
    mod_all = _modulation(jnp.concatenate(cs, axis=0), w_ada, b_ada)
    xs = list(xs)
    for l in range(depth):
        p = _layer_params(l, *weights)
        start = 0
        for i, n in enumerate(sizes):
            mod = mod_all[l, start:start + n].reshape(n, N_MOD, D_MODEL)
            xs[i] = _layer(xs[i], mod, p)
            start += n
    return tuple(xs)


def kernel(x_prompt, x_sample, c_prompt, c_sample, w_ada, b_ada, g_pre_mix, g_post_mix, g_pre_ffn, g_post_ffn, w_in, w_s, b_s, g_vn, g_out_a, w_gf, b_gf, w_gb, b_gb, g_out_b, w_out, w_up, w_conv, b_conv, w_down):
    return _trunks((x_prompt, x_sample), (c_prompt, c_sample), w_ada, b_ada, g_pre_mix, g_post_mix, g_pre_ffn,
                   g_post_ffn, w_in, w_s, b_s, g_vn, g_out_a, w_gf, b_gf, w_gb, b_gb, g_out_b, w_out, w_up, w_conv,
                   b_conv, w_down)
```

```python
import functools

import jax
import jax.numpy as jnp
from jax import lax
from jax.experimental import pallas as pl
from jax.experimental.pallas import tpu as pltpu

F32 = jnp.float32
BF16 = jnp.bfloat16

D_MODEL = 1024
N_MOD = 6
EPS = 1e-6
SG_HEADS = 4
SG_WIDTH = 512
SG_HEAD_DIM = 128
SG_CHUNK = 128
GLA_HEADS = 4
GLA_DK = 64
GLA_DV = 128
GLA_K_WIDTH = 256
GLA_V_WIDTH = 512
GLA_GATE_RANK = 16
GLA_GATE_NORMALIZER = 16.0
GLA_CHUNK = 128
COL_U, COL_VA, COL_Q, COL_V, COL_R, COL_LR, COL_END = 0, 512, 1024, 1536, 2048, 2560, 2592
LR_PAD = 128
D_FF = 2816
FF_BLOCK = 256
CONV_HALO = 8

MIX_TILE = 1024
FFN_TILE = 1024
VMEM_LIMIT = 56 * 1024 * 1024


def _sigmoid(x):
    return 1.0 / (1.0 + jnp.exp(-x))


def _log_sigmoid(x):
    return jnp.minimum(x, 0.0) - jnp.log(1.0 + jnp.exp(-jnp.abs(x)))


def _rms_scale(x):
    return x * lax.rsqrt(jnp.mean(x * x, axis=-1, keepdims=True) + EPS)


def _adaln(x, gain_row, scale_row, shift_row):
    return _rms_scale(x) * (gain_row * (1.0 + scale_row)) + shift_row


def _const_spec(shape):
    nd = len(shape)
    return pl.BlockSpec(shape, lambda *_: (0,) * nd, pipeline_mode=pl.Buffered(1))


def _mod_kernel(c_ref, w_ref, b_ref, o_ref):
    c = c_ref[...]
    s = (c * _sigmoid(c)).astype(BF16)
    o_ref[0] = jnp.dot(s, w_ref[0].astype(BF16), preferred_element_type=F32) + b_ref[0]


def _modulation(c_all, w_ada, b_ada):
    depth = w_ada.shape[0]
    rows = c_all.shape[0]
    cols = w_ada.shape[2]
    bn = 1536
    return pl.pallas_call(
        _mod_kernel,
        grid=(depth, cols // bn),
        in_specs=[
            pl.BlockSpec((rows, D_MODEL), lambda l, j: (0, 0)),
            pl.BlockSpec((1, D_MODEL, bn), lambda l, j: (l, 0, j)),
            pl.BlockSpec((1, 1, bn), lambda l, j: (l, 0, j)),
        ],
        out_specs=pl.BlockSpec((1, rows, bn), lambda l, j: (l, 0, j)),
        out_shape=jax.ShapeDtypeStruct((depth, rows, cols), F32),
        compiler_params=pltpu.CompilerParams(vmem_limit_bytes=VMEM_LIMIT),
        name="adaln_mod",
    )(c_all, w_ada, b_ada.reshape(depth, 1, cols))


def _gla_tile(qk_ref, v_ref, gate_ref, s_ref, write_o, reverse):
    tt = gate_ref.shape[0]
    c = GLA_CHUNK
    n_c = tt // c
    n_p = GLA_HEADS // 2
    kw = GLA_K_WIDTH
    pk, pv = 2 * GLA_DK, 2 * GLA_DV
    row = lax.broadcasted_iota(jnp.int32, (c, c), 0)
    col = lax.broadcasted_iota(jnp.int32, (c, c), 1)
    causal = (col >= row) if reverse else (col <= row)
    tri = jnp.where(causal, 1.0, 0.0).astype(BF16)
    causal2 = jnp.concatenate([causal, causal], axis=1)
    first_k = lax.broadcasted_iota(jnp.int32, (c, pk), 1) < GLA_DK
    first_v = lax.broadcasted_iota(jnp.int32, (c, pv), 1) < GLA_DV
    diag = (lax.broadcasted_iota(jnp.int32, (pk, pv), 0) < GLA_DK) == (lax.broadcasted_iota(jnp.int32, (pk, pv), 1) < GLA_DV)
    rep_row = lax.broadcasted_iota(jnp.int32, (16, kw), 0)
    ones16 = jnp.ones((16, pv), BF16)
    zero_k = jnp.zeros((c, pk), BF16)
    zero_v = jnp.zeros((c, pv), BF16)
    tn = (((0,), (0,)), ((), ()))
    nt = (((1,), (1,)), ((), ()))
    chunks = range(n_c)
    rows = [slice(ci * c, (ci + 1) * c) for ci in chunks]

    g = _log_sigmoid(gate_ref[...]) * (1.0 / GLA_GATE_NORMALIZER)
    g_hi = g.astype(BF16)
    g_lo = (g - g_hi.astype(F32)).astype(BF16)
    g_cat = jnp.concatenate([part[r] for r in rows for part in (g_hi, g_lo)], axis=1)
    bb = jnp.dot(tri, g_cat, preferred_element_type=F32)
    b = [bb[:, 2 * ci * kw:(2 * ci + 1) * kw] + bb[:, (2 * ci + 1) * kw:(2 * ci + 2) * kw] for ci in chunks]
    b_mid = [x[c // 2:c // 2 + 1] if reverse else x[c // 2 - 1:c // 2] for x in b]
    b_last = [x[0:1] if reverse else x[c - 1:c] for x in b]

    q_in, q_dec, k_dec, k_end, reps = [], [], [], [], []
    for ci in chunks:
        qk = qk_ref[rows[ci], :].astype(F32)
        q = qk[:, :kw] * (GLA_DK ** -0.5)
        k = qk[:, kw:]
        q_in.append((q * jnp.exp(b[ci])).astype(BF16))
        q_dec.append((q * jnp.exp(b[ci] - b_mid[ci])).astype(BF16))
        k_dec.append((k * jnp.exp(b_mid[ci] - b[ci])).astype(BF16))
        k_end.append((k * jnp.exp(b_last[ci] - b[ci])).astype(BF16))
        bl_hi = b_last[ci].astype(BF16).astype(F32)
        reps.append(jnp.where(rep_row == 0, bl_hi, jnp.where(rep_row == 1, b_last[ci] - bl_hi, 0.0)).astype(BF16))

    pairs = [(ci, p) for ci in chunks for p in range(n_p)]
    ks = [slice(p * pk, (p + 1) * pk) for p in range(n_p)]
    vp = {(ci, p): v_ref[rows[ci], p * pv:(p + 1) * pv] for ci, p in pairs}
    scores, d_s, decay = {}, {}, {}
    for ci, p in pairs:
        kp = k_dec[ci][:, ks[p]]
        k_bd = jnp.concatenate([jnp.where(first_k, kp, zero_k), jnp.where(first_k, zero_k, kp)], axis=0)
        scores[ci, p] = lax.dot_general(q_dec[ci][:, ks[p]], k_bd, nt, preferred_element_type=F32)
    for ci, p in pairs:
        d_s[ci, p] = lax.dot_general(k_end[ci][:, ks[p]], vp[ci, p], tn, preferred_element_type=F32)
        decay[ci, p] = lax.dot_general(reps[ci][:, ks[p]], ones16, tn, preferred_element_type=F32)
    o = {}
    for ci, p in pairs:
        v_bd = jnp.concatenate([jnp.where(first_v, vp[ci, p], zero_v), jnp.where(first_v, zero_v, vp[ci, p])], axis=0)
        masked = jnp.where(causal2, scores[ci, p], 0.0).astype(BF16)
        o[ci, p] = jnp.dot(masked, v_bd, preferred_element_type=F32)
    states = {}
    for p in range(n_p):
        s = s_ref[p]
        for ci in (reversed(chunks) if reverse else chunks):
            states[ci, p] = s.astype(BF16)
            s = jnp.exp(decay[ci, p]) * s + jnp.where(diag, d_s[ci, p], 0.0)
        s_ref[p] = s
    for ci in chunks:
        write_o(rows[ci], jnp.concatenate(
            [o[ci, p] + jnp.dot(q_in[ci][:, ks[p]], states[ci, p], preferred_element_type=F32) for p in range(n_p)],
            axis=1))


def _mix_fwd_kernel(x_ref, mod_ref, gpre_ref, wmain_ref, wlr_ref, wg_ref, bg_ref, ws_ref, bs_ref, gvn_ref, goa_ref,
                    a_ref, qk_ref, v_ref, r_ref, lr_ref, of_ref,
                    s_ref, qkf_ref, gate_ref):
    tt = x_ref.shape[1]

    @pl.when(pl.program_id(1) == 0)
    def _():
        s_ref[...] = jnp.zeros_like(s_ref)

    mod = mod_ref[0]
    hb = _adaln(x_ref[0], gpre_ref[...], mod[1:2], mod[0:1]).astype(BF16)
    lr = jnp.dot(hb, wlr_ref[...], preferred_element_type=F32).astype(BF16)
    proj = jnp.dot(hb, wmain_ref[...], preferred_element_type=F32)
    lr_ref[0] = lr
    gate_ref[...] = jnp.dot(lr, wg_ref[:, :GLA_K_WIDTH], preferred_element_type=F32) + bg_ref[:, :GLA_K_WIDTH]
    qkf_ref[...] = proj[:, COL_Q:COL_V]
    qk_ref[0] = proj[:, COL_Q:COL_V].astype(BF16)
    v_ref[0] = proj[:, COL_V:COL_R].astype(BF16)
    r_ref[0] = proj[:, COL_R:COL_LR].astype(BF16)

    n_sg = tt // SG_CHUNK
    heads = [slice(hd * SG_HEAD_DIM, (hd + 1) * SG_HEAD_DIM) for hd in range(SG_HEADS)]
    vn_cat = []
    for cs in heads:
        vh = proj[:, COL_VA + cs.start:COL_VA + cs.stop]
        xc = vh - jnp.mean(vh, axis=-1, keepdims=True)
        vn = (xc * lax.rsqrt(jnp.mean(xc * xc, axis=-1, keepdims=True) + EPS) * gvn_ref[:, cs]).astype(BF16)
        vn_cat.append(jnp.concatenate([vn[i * SG_CHUNK:(i + 1) * SG_CHUNK] for i in range(n_sg)], axis=1))
    mixed_cat = [jnp.dot(ws_ref[hd], vn_cat[hd], preferred_element_type=F32) for hd in range(SG_HEADS)]
    a_heads = []
    for hd, cs in enumerate(heads):
        mixed = jnp.concatenate(
            [mixed_cat[hd][:, i * SG_CHUNK:(i + 1) * SG_CHUNK] + bs_ref[hd] for i in range(n_sg)], axis=0)
        ah = proj[:, COL_U + cs.start:COL_U + cs.stop] * mixed
        a_heads.append((_rms_scale(ah) * goa_ref[:, cs]).astype(BF16))
    a_ref[0] = jnp.concatenate(a_heads, axis=1)

    def write_o(rows, o):
        of_ref[0, rows, :] = o.astype(of_ref.dtype)

    _gla_tile(qkf_ref, v_ref.at[0], gate_ref, s_ref, write_o, False)


def _mix_fwd(x, mod, g_pre, w_main, w_lr, w_g, b_g, w_s, bs_full, g_vn, g_out_a):
    bsz, seq, _ = x.shape
    tt = min(MIX_TILE, seq)
    n_t = seq // tt
    tok = lambda w: pl.BlockSpec((1, tt, w), lambda b, t: (b, t, 0))
    out_w = (SG_WIDTH, 2 * GLA_K_WIDTH, GLA_V_WIDTH, GLA_V_WIDTH, LR_PAD, GLA_V_WIDTH)
    return pl.pallas_call(
        _mix_fwd_kernel,
        grid=(bsz, n_t),
        in_specs=[
            tok(D_MODEL),
            pl.BlockSpec((1, N_MOD, D_MODEL), lambda b, t: (b, 0, 0)),
            _const_spec((1, D_MODEL)),
            _const_spec((D_MODEL, COL_LR)),
            _const_spec((D_MODEL, LR_PAD)),
            _const_spec((LR_PAD, 2 * GLA_K_WIDTH)),
            _const_spec((1, 2 * GLA_K_WIDTH)),
            _const_spec((SG_HEADS, SG_CHUNK, SG_CHUNK)),
            _const_spec((SG_HEADS, SG_CHUNK, SG_HEAD_DIM)),
            _const_spec((1, SG_WIDTH)),
            _const_spec((1, SG_WIDTH)),
        ],
        out_specs=[tok(w) for w in out_w],
        out_shape=[jax.ShapeDtypeStruct((bsz, seq, w), BF16) for w in out_w],
        scratch_shapes=[
            pltpu.VMEM((GLA_HEADS // 2, 2 * GLA_DK, 2 * GLA_DV), F32),
            pltpu.VMEM((tt, 2 * GLA_K_WIDTH), F32),
            pltpu.VMEM((tt, GLA_K_WIDTH), F32),
        ],
        compiler_params=pltpu.CompilerParams(
            dimension_semantics=("arbitrary", "arbitrary"), vmem_limit_bytes=VMEM_LIMIT),
        name="mix_fwd",
    )(x, mod, g_pre, w_main, w_lr, w_g, b_g, w_s, bs_full, g_vn, g_out_a)


def _mix_bwd_kernel(x_ref, mod_ref, a_ref, qk_ref, v_ref, r_ref, lr_ref, of_ref, wg_ref, bg_ref, gob_ref, wout_ref,
                    gpost_ref, y_ref, s_ref, gate_ref, o_ref):
    tt = x_ref.shape[1]

    @pl.when(pl.program_id(1) == 0)
    def _():
        s_ref[...] = jnp.zeros_like(s_ref)

    gate_ref[...] = jnp.dot(lr_ref[0], wg_ref[:, GLA_K_WIDTH:], preferred_element_type=F32) + bg_ref[:, GLA_K_WIDTH:]

    def write_o(rows, o):
        o_ref[rows, :] = o + of_ref[0, rows, :].astype(F32)

    _gla_tile(qk_ref.at[0], v_ref.at[0], gate_ref, s_ref, write_o, True)

    r = r_ref[0].astype(F32)
    swish = r * _sigmoid(r)
    o_heads = []
    for hd in range(GLA_HEADS):
        vs = slice(hd * GLA_DV, (hd + 1) * GLA_DV)
        o_heads.append((_rms_scale(o_ref[:, vs]) * (gob_ref[:, vs] * swish[:, vs])).astype(BF16))
    cat = jnp.concatenate([a_ref[0]] + o_heads, axis=1)
    y = jnp.dot(cat, wout_ref[...], preferred_element_type=F32)
    mod = mod_ref[0]
    y_ref[0] = x_ref[0] + _rms_scale(y) * (mod[2:3] * gpost_ref[...])


def _mix_bwd(x, mod, a, qk, v, r, lr, o_fwd, w_g, b_g, g_out_b, w_out, g_post):
    bsz, seq, _ = x.shape
    tt = min(MIX_TILE, seq)
    n_t = seq // tt
    tok = lambda w: pl.BlockSpec((1, tt, w), lambda b, t: (b, n_t - 1 - t, 0))
    return pl.pallas_call(
        _mix_bwd_kernel,
        grid=(bsz, n_t),
        in_specs=[
            tok(D_MODEL),
            pl.BlockSpec((1, N_MOD, D_MODEL), lambda b, t: (b, 0, 0)),
            tok(SG_WIDTH), tok(2 * GLA_K_WIDTH), tok(GLA_V_WIDTH), tok(GLA_V_WIDTH), tok(LR_PAD), tok(GLA_V_WIDTH),
            _const_spec((LR_PAD, 2 * GLA_K_WIDTH)),
            _const_spec((1, 2 * GLA_K_WIDTH)),
            _const_spec((1, GLA_V_WIDTH)),
            _const_spec((D_MODEL, D_MODEL)),
            _const_spec((1, D_MODEL)),
        ],
        out_specs=tok(D_MODEL),
        out_shape=jax.ShapeDtypeStruct((bsz, seq, D_MODEL), F32),
        scratch_shapes=[
            pltpu.VMEM((GLA_HEADS // 2, 2 * GLA_DK, 2 * GLA_DV), F32),
            pltpu.VMEM((tt, GLA_K_WIDTH), F32),
            pltpu.VMEM((tt, GLA_V_WIDTH), F32),
        ],
        compiler_params=pltpu.CompilerParams(
            dimension_semantics=("arbitrary", "arbitrary"), vmem_limit_bytes=VMEM_LIMIT),
        name="mix_bwd",
    )(x, mod, a, qk, v, r, lr, o_fwd, w_g, b_g, g_out_b, w_out, g_post)


def _ffn_kernel(x_ref, xp_ref, xn_ref, mod_ref, gpre_ref, wup_ref, wconv_ref, bconv_ref, wdown_ref, gpost_ref,
                y_ref, act_ref):
    tt = x_ref.shape[1]
    pitch = tt // 8
    t = pl.program_id(1)
    n_t = pl.num_programs(1)
    mod = mod_ref[0]
    xq = pltpu.einshape("smd->msd", x_ref[0].reshape(8, pitch, D_MODEL)).reshape(tt, D_MODEL)
    sub = lax.broadcasted_iota(jnp.int32, (8, 1), 0)
    halo = jnp.where(sub == 0, pltpu.roll(xp_ref[0], 1, 0), pltpu.roll(xn_ref[0], 7, 0))
    keep = jnp.logical_or(jnp.logical_and(sub == 0, t > 0), jnp.logical_and(sub == 7, t < n_t - 1))
    hq = _adaln(xq, gpre_ref[...], mod[4:5], mod[3:4]).astype(BF16)
    hh = jnp.where(keep, _adaln(halo, gpre_ref[...], mod[4:5], mod[3:4]), 0.0)
    hb = jnp.concatenate([hq, jnp.concatenate([hh, jnp.zeros_like(hh)], axis=0).astype(BF16)], axis=0)
    for j in range(D_FF // FF_BLOCK):
        conv = []
        for part in range(2):
            c0 = part * D_FF + j * FF_BLOCK
            cols = slice(c0, c0 + FF_BLOCK)
            u = jnp.dot(hb, wup_ref[:, cols], preferred_element_type=F32)
            edge = u[tt:tt + 8]
            before = jnp.concatenate([jnp.where(sub == 0, edge, pltpu.roll(u[tt - 8:tt], 1, 0)), u[0:tt - 8]], axis=0)
            after = jnp.concatenate([u[8:tt], jnp.where(sub == 7, edge, pltpu.roll(u[0:8], 7, 0))], axis=0)
            conv.append(before * wconv_ref[0:1, cols] + u[0:tt] * wconv_ref[1:2, cols] + after * wconv_ref[2:3, cols]
                        + bconv_ref[:, cols])
        gate, val = conv
        act_ref[:, j * FF_BLOCK:(j + 1) * FF_BLOCK] = (gate * _sigmoid(gate) * val).astype(BF16)
    out = jnp.dot(act_ref[...], wdown_ref[...], preferred_element_type=F32)
    yq = xq + _rms_scale(out) * (mod[5:6] * gpost_ref[...])
    y_ref[0] = pltpu.einshape("msd->smd", yq.reshape(pitch, 8, D_MODEL)).reshape(tt, D_MODEL)


def _ffn(x, mod, g_pre, w_up, w_conv, b_conv, w_down, g_post):
    bsz, seq, _ = x.shape
    tt = min(FFN_TILE, seq)
    n_t = seq // tt
    per = tt // CONV_HALO
    n_halo = seq // CONV_HALO
    return pl.pallas_call(
        _ffn_kernel,
        grid=(bsz, n_t),
        in_specs=[
            pl.BlockSpec((1, tt, D_MODEL), lambda b, t: (b, t, 0)),
            pl.BlockSpec((1, CONV_HALO, D_MODEL), lambda b, t: (b, jnp.maximum(t * per - 1, 0), 0)),
            pl.BlockSpec((1, CONV_HALO, D_MODEL), lambda b, t: (b, jnp.minimum((t + 1) * per, n_halo - 1), 0)),
            pl.BlockSpec((1, N_MOD, D_MODEL), lambda b, t: (b, 0, 0)),
            _const_spec((1, D_MODEL)),
            _const_spec((D_MODEL, 2 * D_FF)),
            _const_spec((3, 2 * D_FF)),
            _const_spec((1, 2 * D_FF)),
            _const_spec((D_FF, D_MODEL)),
            _const_spec((1, D_MODEL)),
        ],
        out_specs=pl.BlockSpec((1, tt, D_MODEL), lambda b, t: (b, t, 0)),
        out_shape=jax.ShapeDtypeStruct((bsz, seq, D_MODEL), F32),
        scratch_shapes=[pltpu.VMEM((tt, D_FF), BF16)],
        compiler_params=pltpu.CompilerParams(
            dimension_semantics=("arbitrary", "arbitrary"), vmem_limit_bytes=VMEM_LIMIT),
        name="ffn",
    )(x, x, x, mod, g_pre, w_up, w_conv, b_conv, w_down, g_post)


def _layer_params(l, g_pre_mix, g_post_mix, g_pre_ffn, g_post_ffn, w_in, w_s, b_s, g_vn, g_out_a, w_gf, b_gf, w_gb,
                  b_gb, g_out_b, w_out, w_up, w_conv, b_conv, w_down):
    row = lambda a: a[l].reshape(1, -1)
    w_lr = jnp.pad(w_in[l][:, COL_LR:COL_END], ((0, 0), (0, LR_PAD - (COL_END - COL_LR)))).astype(BF16)
    w_g = jnp.zeros((LR_PAD, 2 * GLA_K_WIDTH), F32)
    w_g = w_g.at[:GLA_GATE_RANK, :GLA_K_WIDTH].set(w_gf[l])
    w_g = w_g.at[GLA_GATE_RANK:2 * GLA_GATE_RANK, GLA_K_WIDTH:].set(w_gb[l]).astype(BF16)
    b_g = jnp.concatenate([b_gf[l], b_gb[l]]).reshape(1, -1)
    bs_full = jnp.broadcast_to(b_s[l][:, :, None], (SG_HEADS, SG_CHUNK, SG_HEAD_DIM))
    return dict(
        g_pre_mix=row(g_pre_mix), g_post_mix=row(g_post_mix), g_pre_ffn=row(g_pre_ffn), g_post_ffn=row(g_post_ffn),
        w_main=w_in[l][:, :COL_LR].astype(BF16), w_lr=w_lr, w_g=w_g, b_g=b_g,
        w_s=w_s[l].astype(BF16), bs_full=bs_full, g_vn=row(g_vn), g_out_a=row(g_out_a), g_out_b=row(g_out_b),
        w_out=w_out[l].astype(BF16), w_up=w_up[l].astype(BF16), w_conv=w_conv[l], b_conv=row(b_conv),
        w_down=w_down[l].astype(BF16))


def _layer(x, mod, p):
    a, qk, v, r, lr, o_fwd = _mix_fwd(x, mod, p["g_pre_mix"], p["w_main"], p["w_lr"], p["w_g"], p["b_g"], p["w_s"],
                                      p["bs_full"], p["g_vn"], p["g_out_a"])
    x = _mix_bwd(x, mod, a, qk, v, r, lr, o_fwd, p["w_g"], p["b_g"], p["g_out_b"], p["w_out"], p["g_post_mix"])
    return _ffn(x, mod, p["g_pre_ffn"], p["w_up"], p["w_conv"], p["b_conv"], p["w_down"], p["g_post_ffn"])


def _trunks(xs, cs, w_ada, b_ada, *weights):
    depth = w_ada.shape[0]
    sizes = [c.shape[0] for c in cs]
    mod_all = _modulation(jnp.concatenate(cs, axis=0), w_ada, b_ada)
    xs = list(xs)
    for l in range(depth):
        p = _layer_params(l, *weights)
        start = 0
        for i, n in enumerate(sizes):
            mod = mod_all[l, start:start + n].reshape(n, N_MOD, D_MODEL)
            xs[i] = _layer(xs[i], mod, p)
            start += n
    return tuple(xs)


def kernel(x_prompt, x_sample, c_prompt, c_sample, w_ada, b_ada, g_pre_mix, g_post_mix, g_pre_ffn, g_post_ffn, w_in, w_s, b_s, g_vn, g_out_a, w_gf, b_gf, w_gb, b_gb, g_out_b, w_out, w_up, w_conv, b_conv, w_down):
    return _trunks((x_prompt, x_sample), (c_prompt, c_sample), w_ada, b_ada, g_pre_mix, g_post_mix, g_pre_ffn,
                   g_post_ffn, w_in, w_s, b_s, g_vn, g_out_a, w_gf, b_gf, w_gb, b_gb, g_out_b, w_out, w_up, w_conv,
                   b_conv, w_down)
```

```python
import functools

import jax
import jax.numpy as jnp
from jax import lax
from jax.experimental import pallas as pl
from jax.experimental.pallas import tpu as pltpu

F32 = jnp.float32
BF16 = jnp.bfloat16

D_MODEL = 1024
N_MOD = 6
EPS = 1e-6
SG_HEADS = 4
SG_WIDTH = 512
SG_HEAD_DIM = 128
SG_CHUNK = 128
GLA_HEADS = 4
GLA_DK = 64
GLA_DV = 128
GLA_K_WIDTH = 256
GLA_V_WIDTH = 512
GLA_GATE_RANK = 16
GLA_GATE_NORMALIZER = 16.0
GLA_CHUNK = 128
COL_U, COL_VA, COL_Q, COL_V, COL_R, COL_LR, COL_END = 0, 512, 1024, 1536, 2048, 2560, 2592
LR_PAD = 128
D_FF = 2816
FF_BLOCK = 256
CONV_HALO = 8

PK_A, PK_QK, PK_V, PK_R, PK_LR, PK_OF, PK_END = 0, 512, 1024, 1536, 2048, 2176, 2688

MIX_TILE = 1024
FFN_TILE = 1024
VMEM_LIMIT = 56 * 1024 * 1024


def _sigmoid(x):
    return 1.0 / (1.0 + jnp.exp(-x))


LOG2_E = 1.4426950408889634


def _log2_sigmoid_scaled(x, scale):
    return jnp.minimum(x, 0.0) * (scale * LOG2_E) - jnp.log2(1.0 + jnp.exp2(jnp.abs(x) * -LOG2_E)) * scale


def _rms_scale(x):
    return x * lax.rsqrt(jnp.mean(x * x, axis=-1, keepdims=True) + EPS)


def _adaln(x, gain_row, scale_row, shift_row):
    return _rms_scale(x) * (gain_row * (1.0 + scale_row)) + shift_row


def _const_spec(shape):
    nd = len(shape)
    return pl.BlockSpec(shape, lambda *_: (0,) * nd, pipeline_mode=pl.Buffered(1))


def _mod_kernel(c_ref, w_ref, b_ref, o_ref):
    c = c_ref[...]
    s = (c * _sigmoid(c)).astype(BF16)
    o_ref[0] = jnp.dot(s, w_ref[0].astype(BF16), preferred_element_type=F32) + b_ref[0]


def _modulation(c_all, w_ada, b_ada):
    depth = w_ada.shape[0]
    rows = c_all.shape[0]
    cols = w_ada.shape[2]
    bn = 1536
    return pl.pallas_call(
        _mod_kernel,
        grid=(depth, cols // bn),
        in_specs=[
            pl.BlockSpec((rows, D_MODEL), lambda l, j: (0, 0)),
            pl.BlockSpec((1, D_MODEL, bn), lambda l, j: (l, 0, j)),
            pl.BlockSpec((1, 1, bn), lambda l, j: (l, 0, j)),
        ],
        out_specs=pl.BlockSpec((1, rows, bn), lambda l, j: (l, 0, j)),
        out_shape=jax.ShapeDtypeStruct((depth, rows, cols), F32),
        compiler_params=pltpu.CompilerParams(vmem_limit_bytes=VMEM_LIMIT),
        name="adaln_mod",
    )(c_all, w_ada, b_ada.reshape(depth, 1, cols))


def _gla_tile(qk_ref, qk_col, v_ref, v_col, gate_ref, s_ref, write_o, reverse):
    tt = gate_ref.shape[0]
    c = GLA_CHUNK
    n_c = tt // c
    n_p = GLA_HEADS // 2
    kw = GLA_K_WIDTH
    pk, pv = 2 * GLA_DK, 2 * GLA_DV
    row = lax.broadcasted_iota(jnp.int32, (c, c), 0)
    col = lax.broadcasted_iota(jnp.int32, (c, c), 1)
    causal = (col >= row) if reverse else (col <= row)
    tri = jnp.where(causal, 1.0, 0.0).astype(BF16)
    causal2 = jnp.concatenate([causal, causal], axis=1)
    first_k = lax.broadcasted_iota(jnp.int32, (c, pk), 1) < GLA_DK
    first_v = lax.broadcasted_iota(jnp.int32, (c, pv), 1) < GLA_DV
    diag = (lax.broadcasted_iota(jnp.int32, (pk, pv), 0) < GLA_DK) == (lax.broadcasted_iota(jnp.int32, (pk, pv), 1) < GLA_DV)
    rep_row = lax.broadcasted_iota(jnp.int32, (16, kw), 0)
    ones16 = jnp.ones((16, pv), BF16)
    zero_k = jnp.zeros((c, pk), BF16)
    zero_v = jnp.zeros((c, pv), BF16)
    tn = (((0,), (0,)), ((), ()))
    nt = (((1,), (1,)), ((), ()))
    chunks = range(n_c)
    rows = [slice(ci * c, (ci + 1) * c) for ci in chunks]

    g = _log2_sigmoid_scaled(gate_ref[...], 1.0 / GLA_GATE_NORMALIZER)
    g_hi = g.astype(BF16)
    g_lo = (g - g_hi.astype(F32)).astype(BF16)
    g_cat = jnp.concatenate([part[r] for r in rows for part in (g_hi, g_lo)], axis=1)
    bb = jnp.dot(tri, g_cat, preferred_element_type=F32)
    b = [bb[:, 2 * ci * kw:(2 * ci + 1) * kw] + bb[:, (2 * ci + 1) * kw:(2 * ci + 2) * kw] for ci in chunks]
    b_mid = [x[c // 2:c // 2 + 1] if reverse else x[c // 2 - 1:c // 2] for x in b]
    b_last = [x[0:1] if reverse else x[c - 1:c] for x in b]

    q_in, q_dec, k_dec, k_end, reps = [], [], [], [], []
    for ci in chunks:
        qk = qk_ref[rows[ci], qk_col:qk_col + 2 * kw].astype(F32)
        q = qk[:, :kw] * (GLA_DK ** -0.5)
        k = qk[:, kw:]
        q_in.append((q * jnp.exp2(b[ci])).astype(BF16))
        q_dec.append((q * jnp.exp2(b[ci] - b_mid[ci])).astype(BF16))
        k_dec.append((k * jnp.exp2(b_mid[ci] - b[ci])).astype(BF16))
        k_end.append((k * jnp.exp2(b_last[ci] - b[ci])).astype(BF16))
        bl_hi = b_last[ci].astype(BF16).astype(F32)
        reps.append(jnp.where(rep_row == 0, bl_hi, jnp.where(rep_row == 1, b_last[ci] - bl_hi, 0.0)).astype(BF16))

    pairs = [(ci, p) for ci in chunks for p in range(n_p)]
    ks = [slice(p * pk, (p + 1) * pk) for p in range(n_p)]
    vp = {(ci, p): v_ref[rows[ci], v_col + p * pv:v_col + (p + 1) * pv] for ci, p in pairs}
    scores, d_s, decay = {}, {}, {}
    for ci, p in pairs:
        kp = k_dec[ci][:, ks[p]]
        k_bd = jnp.concatenate([jnp.where(first_k, kp, zero_k), jnp.where(first_k, zero_k, kp)], axis=0)
        scores[ci, p] = lax.dot_general(q_dec[ci][:, ks[p]], k_bd, nt, preferred_element_type=F32)
    for ci, p in pairs:
        d_s[ci, p] = lax.dot_general(k_end[ci][:, ks[p]], vp[ci, p], tn, preferred_element_type=F32)
        decay[ci, p] = lax.dot_general(reps[ci][:, ks[p]], ones16, tn, preferred_element_type=F32)
    o = {}
    for ci, p in pairs:
        v_bd = jnp.concatenate([jnp.where(first_v, vp[ci, p], zero_v), jnp.where(first_v, zero_v, vp[ci, p])], axis=0)
        masked = jnp.where(causal2, scores[ci, p], 0.0).astype(BF16)
        o[ci, p] = jnp.dot(masked, v_bd, preferred_element_type=F32)
    states = {}
    for p in range(n_p):
        s = s_ref[p]
        for ci in (reversed(chunks) if reverse else chunks):
            states[ci, p] = s.astype(BF16)
            s = jnp.exp2(decay[ci, p]) * s + jnp.where(diag, d_s[ci, p], 0.0)
        s_ref[p] = s
    for ci in chunks:
        write_o(rows[ci], jnp.concatenate(
            [o[ci, p] + jnp.dot(q_in[ci][:, ks[p]], states[ci, p], preferred_element_type=F32) for p in range(n_p)],
            axis=1))


def _mix_fwd_kernel(x_ref, mod_ref, gpre_ref, wmain_ref, wlr_ref, wg_ref, bg_ref, ws_ref, bs_ref, gvn_ref, goa_ref,
                    pk_ref,
                    s_ref, qkf_ref, gate_ref):
    tt = x_ref.shape[1]

    @pl.when(pl.program_id(1) == 0)
    def _():
        s_ref[...] = jnp.zeros_like(s_ref)

    mod = mod_ref[0]
    hb = _adaln(x_ref[0], gpre_ref[...], mod[1:2], mod[0:1]).astype(BF16)
    lr = jnp.dot(hb, wlr_ref[...], preferred_element_type=F32).astype(BF16)
    proj = jnp.dot(hb, wmain_ref[...], preferred_element_type=F32)
    pk_ref[0, :, PK_LR:PK_OF] = lr
    gate_ref[...] = jnp.dot(lr, wg_ref[:, :GLA_K_WIDTH], preferred_element_type=F32) + bg_ref[:, :GLA_K_WIDTH]
    qkf_ref[...] = proj[:, COL_Q:COL_V]
    pk_ref[0, :, PK_QK:PK_LR] = proj[:, COL_Q:COL_LR].astype(BF16)

    n_sg = tt // SG_CHUNK
    heads = [slice(hd * SG_HEAD_DIM, (hd + 1) * SG_HEAD_DIM) for hd in range(SG_HEADS)]
    vn_cat = []
    for cs in heads:
        vh = proj[:, COL_VA + cs.start:COL_VA + cs.stop]
        xc = vh - jnp.mean(vh, axis=-1, keepdims=True)
        vn = (xc * lax.rsqrt(jnp.mean(xc * xc, axis=-1, keepdims=True) + EPS) * gvn_ref[:, cs]).astype(BF16)
        vn_cat.append(jnp.concatenate([vn[i * SG_CHUNK:(i + 1) * SG_CHUNK] for i in range(n_sg)], axis=1))
    mixed_cat = [jnp.dot(ws_ref[hd], vn_cat[hd], preferred_element_type=F32) for hd in range(SG_HEADS)]
    a_heads = []
    for hd, cs in enumerate(heads):
        mixed = jnp.concatenate(
            [mixed_cat[hd][:, i * SG_CHUNK:(i + 1) * SG_CHUNK] + bs_ref[hd] for i in range(n_sg)], axis=0)
        ah = proj[:, COL_U + cs.start:COL_U + cs.stop] * mixed
        a_heads.append((_rms_scale(ah) * goa_ref[:, cs]).astype(BF16))
    pk_ref[0, :, PK_A:PK_QK] = jnp.concatenate(a_heads, axis=1)

    def write_o(rows, o):
        pk_ref[0, rows, PK_OF:PK_END] = o.astype(BF16)

    _gla_tile(qkf_ref, 0, pk_ref.at[0], PK_V, gate_ref, s_ref, write_o, False)


def _mix_fwd(x, mod, g_pre, w_main, w_lr, w_g, b_g, w_s, bs_full, g_vn, g_out_a):
    bsz, seq, _ = x.shape
    tt = min(MIX_TILE, seq)
    n_t = seq // tt
    tok = lambda w: pl.BlockSpec((1, tt, w), lambda b, t: (b, t, 0))
    return pl.pallas_call(
        _mix_fwd_kernel,
        grid=(bsz, n_t),
        in_specs=[
            tok(D_MODEL),
            pl.BlockSpec((1, N_MOD, D_MODEL), lambda b, t: (b, 0, 0)),
            _const_spec((1, D_MODEL)),
            _const_spec((D_MODEL, COL_LR)),
            _const_spec((D_MODEL, LR_PAD)),
            _const_spec((LR_PAD, 2 * GLA_K_WIDTH)),
            _const_spec((1, 2 * GLA_K_WIDTH)),
            _const_spec((SG_HEADS, SG_CHUNK, SG_CHUNK)),
            _const_spec((SG_HEADS, SG_CHUNK, SG_HEAD_DIM)),
            _const_spec((1, SG_WIDTH)),
            _const_spec((1, SG_WIDTH)),
        ],
        out_specs=tok(PK_END),
        out_shape=jax.ShapeDtypeStruct((bsz, seq, PK_END), BF16),
        scratch_shapes=[
            pltpu.VMEM((GLA_HEADS // 2, 2 * GLA_DK, 2 * GLA_DV), F32),
            pltpu.VMEM((tt, 2 * GLA_K_WIDTH), F32),
            pltpu.VMEM((tt, GLA_K_WIDTH), F32),
        ],
        compiler_params=pltpu.CompilerParams(
            dimension_semantics=("arbitrary", "arbitrary"), vmem_limit_bytes=VMEM_LIMIT),
        name="mix_fwd",
    )(x, mod, g_pre, w_main, w_lr, w_g, b_g, w_s, bs_full, g_vn, g_out_a)


def _mix_bwd_kernel(x_ref, mod_ref, pk_ref, wg_ref, bg_ref, gob_ref, wout_ref,
                    gpost_ref, y_ref, s_ref, gate_ref, o_ref):
    tt = x_ref.shape[1]

    @pl.when(pl.program_id(1) == 0)
    def _():
        s_ref[...] = jnp.zeros_like(s_ref)

    gate_ref[...] = (jnp.dot(pk_ref[0, :, PK_LR:PK_OF], wg_ref[:, GLA_K_WIDTH:], preferred_element_type=F32)
                     + bg_ref[:, GLA_K_WIDTH:])

    def write_o(rows, o):
        o_ref[rows, :] = o + pk_ref[0, rows, PK_OF:PK_END].astype(F32)

    _gla_tile(pk_ref.at[0], PK_QK, pk_ref.at[0], PK_V, gate_ref, s_ref, write_o, True)

    r = pk_ref[0, :, PK_R:PK_LR].astype(F32)
    swish = r * _sigmoid(r)
    o_heads = []
    for hd in range(GLA_HEADS):
        vs = slice(hd * GLA_DV, (hd + 1) * GLA_DV)
        o_heads.append((_rms_scale(o_ref[:, vs]) * (gob_ref[:, vs] * swish[:, vs])).astype(BF16))
    cat = jnp.concatenate([pk_ref[0, :, PK_A:PK_QK]] + o_heads, axis=1)
    y = jnp.dot(cat, wout_ref[...], preferred_element_type=F32)
    mod = mod_ref[0]
    y_ref[0] = x_ref[0] + _rms_scale(y) * (mod[2:3] * gpost_ref[...])


def _mix_bwd(x, mod, packed, w_g, b_g, g_out_b, w_out, g_post):
    bsz, seq, _ = x.shape
    tt = min(MIX_TILE, seq)
    n_t = seq // tt
    tok = lambda w: pl.BlockSpec((1, tt, w), lambda b, t: (b, n_t - 1 - t, 0))
    return pl.pallas_call(
        _mix_bwd_kernel,
        grid=(bsz, n_t),
        in_specs=[
            tok(D_MODEL),
            pl.BlockSpec((1, N_MOD, D_MODEL), lambda b, t: (b, 0, 0)),
            tok(PK_END),
            _const_spec((LR_PAD, 2 * GLA_K_WIDTH)),
            _const_spec((1, 2 * GLA_K_WIDTH)),
            _const_spec((1, GLA_V_WIDTH)),
            _const_spec((D_MODEL, D_MODEL)),
            _const_spec((1, D_MODEL)),
        ],
        out_specs=tok(D_MODEL),
        out_shape=jax.ShapeDtypeStruct((bsz, seq, D_MODEL), F32),
        scratch_shapes=[
            pltpu.VMEM((GLA_HEADS // 2, 2 * GLA_DK, 2 * GLA_DV), F32),
            pltpu.VMEM((tt, GLA_K_WIDTH), F32),
            pltpu.VMEM((tt, GLA_V_WIDTH), F32),
        ],
        compiler_params=pltpu.CompilerParams(
            dimension_semantics=("arbitrary", "arbitrary"), vmem_limit_bytes=VMEM_LIMIT),
        name="mix_bwd",
    )(x, mod, packed, w_g, b_g, g_out_b, w_out, g_post)


def _ffn_kernel(x_ref, xp_ref, xn_ref, mod_ref, gpre_ref, wup_ref, wconv_ref, bconv_ref, wdown_ref, gpost_ref,
                y_ref, act_ref):
    tt = x_ref.shape[1]
    pitch = tt // 8
    t = pl.program_id(1)
    n_t = pl.num_programs(1)
    mod = mod_ref[0]
    xq = pltpu.einshape("smd->msd", x_ref[0].reshape(8, pitch, D_MODEL)).reshape(tt, D_MODEL)
    sub = lax.broadcasted_iota(jnp.int32, (8, 1), 0)
    halo = jnp.where(sub == 0, pltpu.roll(xp_ref[0], 1, 0), pltpu.roll(xn_ref[0], 7, 0))
    keep = jnp.logical_or(jnp.logical_and(sub == 0, t > 0), jnp.logical_and(sub == 7, t < n_t - 1))
    hq = _adaln(xq, gpre_ref[...], mod[4:5], mod[3:4]).astype(BF16)
    hh = jnp.where(keep, _adaln(halo, gpre_ref[...], mod[4:5], mod[3:4]), 0.0)
    hb = jnp.concatenate([hq, jnp.concatenate([hh, jnp.zeros_like(hh)], axis=0).astype(BF16)], axis=0)
    for j in range(D_FF // FF_BLOCK):
        conv = []
        for part in range(2):
            c0 = part * D_FF + j * FF_BLOCK
            cols = slice(c0, c0 + FF_BLOCK)
            u = jnp.dot(hb, wup_ref[:, cols], preferred_element_type=F32)
            edge = u[tt:tt + 8]
            before = jnp.concatenate([jnp.where(sub == 0, edge, pltpu.roll(u[tt - 8:tt], 1, 0)), u[0:tt - 8]], axis=0)
            after = jnp.concatenate([u[8:tt], jnp.where(sub == 7, edge, pltpu.roll(u[0:8], 7, 0))], axis=0)
            conv.append(before * wconv_ref[0:1, cols] + u[0:tt] * wconv_ref[1:2, cols] + after * wconv_ref[2:3, cols]
                        + bconv_ref[:, cols])
        gate, val = conv
        act_ref[:, j * FF_BLOCK:(j + 1) * FF_BLOCK] = (gate * _sigmoid(gate) * val).astype(BF16)
    out = jnp.dot(act_ref[...], wdown_ref[...], preferred_element_type=F32)
    yq = xq + _rms_scale(out) * (mod[5:6] * gpost_ref[...])
    y_ref[0] = pltpu.einshape("msd->smd", yq.reshape(pitch, 8, D_MODEL)).reshape(tt, D_MODEL)


def _ffn(x, mod, g_pre, w_up, w_conv, b_conv, w_down, g_post):
    bsz, seq, _ = x.shape
    tt = min(FFN_TILE, seq)
    n_t = seq // tt
    per = tt // CONV_HALO
    n_halo = seq // CONV_HALO
    return pl.pallas_call(
        _ffn_kernel,
        grid=(bsz, n_t),
        in_specs=[
            pl.BlockSpec((1, tt, D_MODEL), lambda b, t: (b, t, 0)),
            pl.BlockSpec((1, CONV_HALO, D_MODEL), lambda b, t: (b, jnp.maximum(t * per - 1, 0), 0)),
            pl.BlockSpec((1, CONV_HALO, D_MODEL), lambda b, t: (b, jnp.minimum((t + 1) * per, n_halo - 1), 0)),
            pl.BlockSpec((1, N_MOD, D_MODEL), lambda b, t: (b, 0, 0)),
            _const_spec((1, D_MODEL)),
            _const_spec((D_MODEL, 2 * D_FF)),
            _const_spec((3, 2 * D_FF)),
            _const_spec((1, 2 * D_FF)),
            _const_spec((D_FF, D_MODEL)),
            _const_spec((1, D_MODEL)),
        ],
        out_specs=pl.BlockSpec((1, tt, D_MODEL), lambda b, t: (b, t, 0)),
        out_shape=jax.ShapeDtypeStruct((bsz, seq, D_MODEL), F32),
        scratch_shapes=[pltpu.VMEM((tt, D_FF), BF16)],
        compiler_params=pltpu.CompilerParams(
            dimension_semantics=("arbitrary", "arbitrary"), vmem_limit_bytes=VMEM_LIMIT),
        name="ffn",
    )(x, x, x, mod, g_pre, w_up, w_conv, b_conv, w_down, g_post)


def _layer_params(l, g_pre_mix, g_post_mix, g_pre_ffn, g_post_ffn, w_in, w_s, b_s, g_vn, g_out_a, w_gf, b_gf, w_gb,
                  b_gb, g_out_b, w_out, w_up, w_conv, b_conv, w_down):
    row = lambda a: a[l].reshape(1, -1)
    w_lr = jnp.pad(w_in[l][:, COL_LR:COL_END], ((0, 0), (0, LR_PAD - (COL_END - COL_LR)))).astype(BF16)
    w_g = jnp.zeros((LR_PAD, 2 * GLA_K_WIDTH), F32)
    w_g = w_g.at[:GLA_GATE_RANK, :GLA_K_WIDTH].set(w_gf[l])
    w_g = w_g.at[GLA_GATE_RANK:2 * GLA_GATE_RANK, GLA_K_WIDTH:].set(w_gb[l]).astype(BF16)
    b_g = jnp.concatenate([b_gf[l], b_gb[l]]).reshape(1, -1)
    bs_full = jnp.broadcast_to(b_s[l][:, :, None], (SG_HEADS, SG_CHUNK, SG_HEAD_DIM))
    return dict(
        g_pre_mix=row(g_pre_mix), g_post_mix=row(g_post_mix), g_pre_ffn=row(g_pre_ffn), g_post_ffn=row(g_post_ffn),
        w_main=w_in[l][:, :COL_LR].astype(BF16), w_lr=w_lr, w_g=w_g, b_g=b_g,
        w_s=w_s[l].astype(BF16), bs_full=bs_full, g_vn=row(g_vn), g_out_a=row(g_out_a), g_out_b=row(g_out_b),
        w_out=w_out[l].astype(BF16), w_up=w_up[l].astype(BF16), w_conv=w_conv[l], b_conv=row(b_conv),
        w_down=w_down[l].astype(BF16))


def _layer(x, mod, p):
    packed = _mix_fwd(x, mod, p["g_pre_mix"], p["w_main"], p["w_lr"], p["w_g"], p["b_g"], p["w_s"], p["bs_full"],
                      p["g_vn"], p["g_out_a"])
    x = _mix_bwd(x, mod, packed, p["w_g"], p["b_g"], p["g_out_b"], p["w_out"], p["g_post_mix"])
    return _ffn(x, mod, p["g_pre_ffn"], p["w_up"], p["w_conv"], p["b_conv"], p["w_down"], p["g_post_ffn"])


def _trunks(xs, cs, w_ada, b_ada, *weights):
    depth = w_ada.shape[0]
    sizes = [c.shape[0] for c in cs]
    mod_all = _modulation(jnp.concatenate(cs, axis=0), w_ada, b_ada)
    xs = list(xs)
    for l in range(depth):
        p = _layer_params(l, *weights)
        start = 0
        for i, n in enumerate(sizes):
            mod = mod_all[l, start:start + n].reshape(n, N_MOD, D_MODEL)
            xs[i] = _layer(xs[i], mod, p)
            start += n
    return tuple(xs)


def kernel(x_prompt, x_sample, c_prompt, c_sample, w_ada, b_ada, g_pre_mix, g_post_mix, g_pre_ffn, g_post_ffn, w_in, w_s, b_s, g_vn, g_out_a, w_gf, b_gf, w_gb, b_gb, g_out_b, w_out, w_up, w_conv, b_conv, w_down):
    return _trunks((x_prompt, x_sample), (c_prompt, c_sample), w_ada, b_ada, g_pre_mix, g_post_mix, g_pre_ffn,
                   g_post_ffn, w_in, w_s, b_s, g_vn, g_out_a, w_gf, b_gf, w_gb, b_gb, g_out_b, w_out, w_up, w_conv,
                   b_conv, w_down)
```

```python
import functools

import jax
import jax.numpy as jnp
from jax import lax
from jax.experimental import pallas as pl
from jax.experimental.pallas import tpu as pltpu

F32 = jnp.float32
BF16 = jnp.bfloat16

D_MODEL = 1024
N_MOD = 6
EPS = 1e-6
SG_HEADS = 4
SG_WIDTH = 512
SG_HEAD_DIM = 128
SG_CHUNK = 128
GLA_HEADS = 4
GLA_DK = 64
GLA_DV = 128
GLA_K_WIDTH = 256
GLA_V_WIDTH = 512
GLA_GATE_RANK = 16
GLA_GATE_NORMALIZER = 16.0
GLA_CHUNK = 128
COL_U, COL_VA, COL_Q, COL_V, COL_R, COL_LR, COL_END = 0, 512, 1024, 1536, 2048, 2560, 2592
LR_PAD = 128
D_FF = 2816
FF_BLOCK = 256
CONV_HALO = 8

PK_A, PK_QK, PK_V, PK_R, PK_LR, PK_OF, PK_END = 0, 512, 1024, 1536, 2048, 2176, 2688

MIX_TILE = 1024
FFN_TILE = 1024
VMEM_LIMIT = 56 * 1024 * 1024


def _sigmoid(x):
    return 1.0 / (1.0 + jnp.exp(-x))


LOG2_E = 1.4426950408889634


def _log2_sigmoid_scaled(x, scale):
    return jnp.minimum(x, 0.0) * (scale * LOG2_E) - jnp.log2(1.0 + jnp.exp2(jnp.abs(x) * -LOG2_E)) * scale


def _rms_scale(x):
    return x * lax.rsqrt(jnp.mean(x * x, axis=-1, keepdims=True) + EPS)


def _adaln(x, gain_row, scale_row, shift_row):
    return _rms_scale(x) * (gain_row * (1.0 + scale_row)) + shift_row


def _const_spec(shape):
    nd = len(shape)
    return pl.BlockSpec(shape, lambda *_: (0,) * nd, pipeline_mode=pl.Buffered(1))


def _mod_kernel(c_ref, w_ref, b_ref, o_ref):
    c = c_ref[...]
    s = (c * _sigmoid(c)).astype(BF16)
    o_ref[0] = jnp.dot(s, w_ref[0].astype(BF16), preferred_element_type=F32) + b_ref[0]


def _modulation(c_all, w_ada, b_ada):
    depth = w_ada.shape[0]
    rows = c_all.shape[0]
    cols = w_ada.shape[2]
    bn = 1536
    return pl.pallas_call(
        _mod_kernel,
        grid=(depth, cols // bn),
        in_specs=[
            pl.BlockSpec((rows, D_MODEL), lambda l, j: (0, 0)),
            pl.BlockSpec((1, D_MODEL, bn), lambda l, j: (l, 0, j)),
            pl.BlockSpec((1, 1, bn), lambda l, j: (l, 0, j)),
        ],
        out_specs=pl.BlockSpec((1, rows, bn), lambda l, j: (l, 0, j)),
        out_shape=jax.ShapeDtypeStruct((depth, rows, cols), F32),
        compiler_params=pltpu.CompilerParams(vmem_limit_bytes=VMEM_LIMIT),
        name="adaln_mod",
    )(c_all, w_ada, b_ada.reshape(depth, 1, cols))


def _gla_tile(qk_ref, qk_col, v_ref, v_col, gate_ref, s_ref, write_o, reverse):
    tt = gate_ref.shape[0]
    c = GLA_CHUNK
    n_c = tt // c
    n_p = GLA_HEADS // 2
    kw = GLA_K_WIDTH
    pk, pv = 2 * GLA_DK, 2 * GLA_DV
    row = lax.broadcasted_iota(jnp.int32, (c, c), 0)
    col = lax.broadcasted_iota(jnp.int32, (c, c), 1)
    causal = (col >= row) if reverse else (col <= row)
    tri = jnp.where(causal, 1.0, 0.0).astype(BF16)
    causal2 = jnp.concatenate([causal, causal], axis=1)
    first_k = lax.broadcasted_iota(jnp.int32, (c, pk), 1) < GLA_DK
    first_v = lax.broadcasted_iota(jnp.int32, (c, pv), 1) < GLA_DV
    diag = (lax.broadcasted_iota(jnp.int32, (pk, pv), 0) < GLA_DK) == (lax.broadcasted_iota(jnp.int32, (pk, pv), 1) < GLA_DV)
    rep_row = lax.broadcasted_iota(jnp.int32, (16, kw), 0)
    ones16 = jnp.ones((16, pv), BF16)
    zero_k = jnp.zeros((c, pk), BF16)
    zero_v = jnp.zeros((c, pv), BF16)
    tn = (((0,), (0,)), ((), ()))
    nt = (((1,), (1,)), ((), ()))
    chunks = range(n_c)
    rows = [slice(ci * c, (ci + 1) * c) for ci in chunks]

    g = _log2_sigmoid_scaled(gate_ref[...], 1.0 / GLA_GATE_NORMALIZER)
    g_hi = g.astype(BF16)
    g_lo = (g - g_hi.astype(F32)).astype(BF16)
    g_cat = jnp.concatenate([part[r] for r in rows for part in (g_hi, g_lo)], axis=1)
    bb = jnp.dot(tri, g_cat, preferred_element_type=F32)
    b = [bb[:, 2 * ci * kw:(2 * ci + 1) * kw] + bb[:, (2 * ci + 1) * kw:(2 * ci + 2) * kw] for ci in chunks]
    b_mid = [x[c // 2:c // 2 + 1] if reverse else x[c // 2 - 1:c // 2] for x in b]
    b_last = [x[0:1] if reverse else x[c - 1:c] for x in b]
    yield

    q_in, q_dec, k_dec, k_end, reps = [], [], [], [], []
    for ci in chunks:
        qk = qk_ref[rows[ci], qk_col:qk_col + 2 * kw].astype(F32)
        q = qk[:, :kw] * (GLA_DK ** -0.5)
        k = qk[:, kw:]
        q_in.append((q * jnp.exp2(b[ci])).astype(BF16))
        q_dec.append((q * jnp.exp2(b[ci] - b_mid[ci])).astype(BF16))
        k_dec.append((k * jnp.exp2(b_mid[ci] - b[ci])).astype(BF16))
        k_end.append((k * jnp.exp2(b_last[ci] - b[ci])).astype(BF16))
        bl_hi = b_last[ci].astype(BF16).astype(F32)
        reps.append(jnp.where(rep_row == 0, bl_hi, jnp.where(rep_row == 1, b_last[ci] - bl_hi, 0.0)).astype(BF16))

    pairs = [(ci, p) for ci in chunks for p in range(n_p)]
    ks = [slice(p * pk, (p + 1) * pk) for p in range(n_p)]
    vp = {(ci, p): v_ref[rows[ci], v_col + p * pv:v_col + (p + 1) * pv] for ci, p in pairs}
    scores, d_s, decay = {}, {}, {}
    for ci, p in pairs:
        kp = k_dec[ci][:, ks[p]]
        k_bd = jnp.concatenate([jnp.where(first_k, kp, zero_k), jnp.where(first_k, zero_k, kp)], axis=0)
        scores[ci, p] = lax.dot_general(q_dec[ci][:, ks[p]], k_bd, nt, preferred_element_type=F32)
    for ci, p in pairs:
        d_s[ci, p] = lax.dot_general(k_end[ci][:, ks[p]], vp[ci, p], tn, preferred_element_type=F32)
        decay[ci, p] = lax.dot_general(reps[ci][:, ks[p]], ones16, tn, preferred_element_type=F32)
    o = {}
    for ci, p in pairs:
        v_bd = jnp.concatenate([jnp.where(first_v, vp[ci, p], zero_v), jnp.where(first_v, zero_v, vp[ci, p])], axis=0)
        masked = jnp.where(causal2, scores[ci, p], 0.0).astype(BF16)
        o[ci, p] = jnp.dot(masked, v_bd, preferred_element_type=F32)
    states = {}
    for p in range(n_p):
        s = s_ref[p]
        for ci in (reversed(chunks) if reverse else chunks):
            states[ci, p] = s.astype(BF16)
            s = jnp.exp2(decay[ci, p]) * s + jnp.where(diag, d_s[ci, p], 0.0)
        s_ref[p] = s
    for ci in chunks:
        write_o(rows[ci], jnp.concatenate(
            [o[ci, p] + jnp.dot(q_in[ci][:, ks[p]], states[ci, p], preferred_element_type=F32) for p in range(n_p)],
            axis=1))


def _mix_fwd_kernel(x_ref, mod_ref, gpre_ref, wmain_ref, wlr_ref, wg_ref, bg_ref, ws_ref, bs_ref, gvn_ref, goa_ref,
                    pk_ref,
                    s_ref, qkf_ref, gate_ref):
    tt = x_ref.shape[1]

    @pl.when(pl.program_id(1) == 0)
    def _():
        s_ref[...] = jnp.zeros_like(s_ref)

    mod = mod_ref[0]
    hb = _adaln(x_ref[0], gpre_ref[...], mod[1:2], mod[0:1]).astype(BF16)
    lr = jnp.dot(hb, wlr_ref[...], preferred_element_type=F32).astype(BF16)
    proj = jnp.dot(hb, wmain_ref[...], preferred_element_type=F32)
    pk_ref[0, :, PK_LR:PK_OF] = lr
    gate_ref[...] = jnp.dot(lr, wg_ref[:, :GLA_K_WIDTH], preferred_element_type=F32) + bg_ref[:, :GLA_K_WIDTH]
    qkf_ref[...] = proj[:, COL_Q:COL_V]
    pk_ref[0, :, PK_QK:PK_LR] = proj[:, COL_Q:COL_LR].astype(BF16)

    def write_o(rows, o):
        pk_ref[0, rows, PK_OF:PK_END] = o.astype(BF16)

    gla = _gla_tile(qkf_ref, 0, pk_ref.at[0], PK_V, gate_ref, s_ref, write_o, False)
    next(gla)

    n_sg = tt // SG_CHUNK
    heads = [slice(hd * SG_HEAD_DIM, (hd + 1) * SG_HEAD_DIM) for hd in range(SG_HEADS)]
    vn_cat = []
    for cs in heads:
        vh = proj[:, COL_VA + cs.start:COL_VA + cs.stop]
        xc = vh - jnp.mean(vh, axis=-1, keepdims=True)
        vn = (xc * lax.rsqrt(jnp.mean(xc * xc, axis=-1, keepdims=True) + EPS) * gvn_ref[:, cs]).astype(BF16)
        vn_cat.append(jnp.concatenate([vn[i * SG_CHUNK:(i + 1) * SG_CHUNK] for i in range(n_sg)], axis=1))
    mixed_cat = [jnp.dot(ws_ref[hd], vn_cat[hd], preferred_element_type=F32) for hd in range(SG_HEADS)]
    a_heads = []
    for hd, cs in enumerate(heads):
        mixed = jnp.concatenate(
            [mixed_cat[hd][:, i * SG_CHUNK:(i + 1) * SG_CHUNK] + bs_ref[hd] for i in range(n_sg)], axis=0)
        ah = proj[:, COL_U + cs.start:COL_U + cs.stop] * mixed
        a_heads.append((_rms_scale(ah) * goa_ref[:, cs]).astype(BF16))
    pk_ref[0, :, PK_A:PK_QK] = jnp.concatenate(a_heads, axis=1)

    for _ in gla:
        pass


def _mix_fwd(x, mod, g_pre, w_main, w_lr, w_g, b_g, w_s, bs_full, g_vn, g_out_a):
    bsz, seq, _ = x.shape
    tt = min(MIX_TILE, seq)
    n_t = seq // tt
    tok = lambda w: pl.BlockSpec((1, tt, w), lambda b, t: (b, t, 0))
    return pl.pallas_call(
        _mix_fwd_kernel,
        grid=(bsz, n_t),
        in_specs=[
            tok(D_MODEL),
            pl.BlockSpec((1, N_MOD, D_MODEL), lambda b, t: (b, 0, 0)),
            _const_spec((1, D_MODEL)),
            _const_spec((D_MODEL, COL_LR)),
            _const_spec((D_MODEL, LR_PAD)),
            _const_spec((LR_PAD, 2 * GLA_K_WIDTH)),
            _const_spec((1, 2 * GLA_K_WIDTH)),
            _const_spec((SG_HEADS, SG_CHUNK, SG_CHUNK)),
            _const_spec((SG_HEADS, SG_CHUNK, SG_HEAD_DIM)),
            _const_spec((1, SG_WIDTH)),
            _const_spec((1, SG_WIDTH)),
        ],
        out_specs=tok(PK_END),
        out_shape=jax.ShapeDtypeStruct((bsz, seq, PK_END), BF16),
        scratch_shapes=[
            pltpu.VMEM((GLA_HEADS // 2, 2 * GLA_DK, 2 * GLA_DV), F32),
            pltpu.VMEM((tt, 2 * GLA_K_WIDTH), F32),
            pltpu.VMEM((tt, GLA_K_WIDTH), F32),
        ],
        compiler_params=pltpu.CompilerParams(
            dimension_semantics=("arbitrary", "arbitrary"), vmem_limit_bytes=VMEM_LIMIT),
        name="mix_fwd",
    )(x, mod, g_pre, w_main, w_lr, w_g, b_g, w_s, bs_full, g_vn, g_out_a)


def _mix_bwd_kernel(x_ref, mod_ref, pk_ref, wg_ref, bg_ref, gob_ref, wout_ref,
                    gpost_ref, y_ref, s_ref, gate_ref, o_ref):
    tt = x_ref.shape[1]

    @pl.when(pl.program_id(1) == 0)
    def _():
        s_ref[...] = jnp.zeros_like(s_ref)

    gate_ref[...] = (jnp.dot(pk_ref[0, :, PK_LR:PK_OF], wg_ref[:, GLA_K_WIDTH:], preferred_element_type=F32)
                     + bg_ref[:, GLA_K_WIDTH:])

    def write_o(rows, o):
        o_ref[rows, :] = o + pk_ref[0, rows, PK_OF:PK_END].astype(F32)

    for _ in _gla_tile(pk_ref.at[0], PK_QK, pk_ref.at[0], PK_V, gate_ref, s_ref, write_o, True):
        pass

    r = pk_ref[0, :, PK_R:PK_LR].astype(F32)
    swish = r * _sigmoid(r)
    o_heads = []
    for hd in range(GLA_HEADS):
        vs = slice(hd * GLA_DV, (hd + 1) * GLA_DV)
        o_heads.append((_rms_scale(o_ref[:, vs]) * (gob_ref[:, vs] * swish[:, vs])).astype(BF16))
    cat = jnp.concatenate([pk_ref[0, :, PK_A:PK_QK]] + o_heads, axis=1)
    y = jnp.dot(cat, wout_ref[...], preferred_element_type=F32)
    mod = mod_ref[0]
    y_ref[0] = x_ref[0] + _rms_scale(y) * (mod[2:3] * gpost_ref[...])


def _mix_bwd(x, mod, packed, w_g, b_g, g_out_b, w_out, g_post):
    bsz, seq, _ = x.shape
    tt = min(MIX_TILE, seq)
    n_t = seq // tt
    tok = lambda w: pl.BlockSpec((1, tt, w), lambda b, t: (b, n_t - 1 - t, 0))
    return pl.pallas_call(
        _mix_bwd_kernel,
        grid=(bsz, n_t),
        in_specs=[
            tok(D_MODEL),
            pl.BlockSpec((1, N_MOD, D_MODEL), lambda b, t: (b, 0, 0)),
            tok(PK_END),
            _const_spec((LR_PAD, 2 * GLA_K_WIDTH)),
            _const_spec((1, 2 * GLA_K_WIDTH)),
            _const_spec((1, GLA_V_WIDTH)),
            _const_spec((D_MODEL, D_MODEL)),
            _const_spec((1, D_MODEL)),
        ],
        out_specs=tok(D_MODEL),
        out_shape=jax.ShapeDtypeStruct((bsz, seq, D_MODEL), F32),
        scratch_shapes=[
            pltpu.VMEM((GLA_HEADS // 2, 2 * GLA_DK, 2 * GLA_DV), F32),
            pltpu.VMEM((tt, GLA_K_WIDTH), F32),
            pltpu.VMEM((tt, GLA_V_WIDTH), F32),
        ],
        compiler_params=pltpu.CompilerParams(
            dimension_semantics=("arbitrary", "arbitrary"), vmem_limit_bytes=VMEM_LIMIT),
        name="mix_bwd",
    )(x, mod, packed, w_g, b_g, g_out_b, w_out, g_post)


def _ffn_kernel(x_ref, xp_ref, xn_ref, mod_ref, gpre_ref, wup_ref, wconv_ref, bconv_ref, wdown_ref, gpost_ref,
                y_ref, act_ref):
    tt = x_ref.shape[1]
    pitch = tt // 8
    t = pl.program_id(1)
    n_t = pl.num_programs(1)
    mod = mod_ref[0]
    xq = pltpu.einshape("smd->msd", x_ref[0].reshape(8, pitch, D_MODEL)).reshape(tt, D_MODEL)
    sub = lax.broadcasted_iota(jnp.int32, (8, 1), 0)
    halo = jnp.where(sub == 0, pltpu.roll(xp_ref[0], 1, 0), pltpu.roll(xn_ref[0], 7, 0))
    keep = jnp.logical_or(jnp.logical_and(sub == 0, t > 0), jnp.logical_and(sub == 7, t < n_t - 1))
    hq = _adaln(xq, gpre_ref[...], mod[4:5], mod[3:4]).astype(BF16)
    hh = jnp.where(keep, _adaln(halo, gpre_ref[...], mod[4:5], mod[3:4]), 0.0)
    hb = jnp.concatenate([hq, jnp.concatenate([hh, jnp.zeros_like(hh)], axis=0).astype(BF16)], axis=0)
    for j in range(D_FF // FF_BLOCK):
        conv = []
        for part in range(2):
            c0 = part * D_FF + j * FF_BLOCK
            cols = slice(c0, c0 + FF_BLOCK)
            u = jnp.dot(hb, wup_ref[:, cols], preferred_element_type=F32)
            edge = u[tt:tt + 8]
            before = jnp.concatenate([jnp.where(sub == 0, edge, pltpu.roll(u[tt - 8:tt], 1, 0)), u[0:tt - 8]], axis=0)
            after = jnp.concatenate([u[8:tt], jnp.where(sub == 7, edge, pltpu.roll(u[0:8], 7, 0))], axis=0)
            conv.append(before * wconv_ref[0:1, cols] + u[0:tt] * wconv_ref[1:2, cols] + after * wconv_ref[2:3, cols]
                        + bconv_ref[:, cols])
        gate, val = conv
        act_ref[:, j * FF_BLOCK:(j + 1) * FF_BLOCK] = (gate * _sigmoid(gate) * val).astype(BF16)
    out = jnp.dot(act_ref[...], wdown_ref[...], preferred_element_type=F32)
    yq = xq + _rms_scale(out) * (mod[5:6] * gpost_ref[...])
    y_ref[0] = pltpu.einshape("msd->smd", yq.reshape(pitch, 8, D_MODEL)).reshape(tt, D_MODEL)


def _ffn(x, mod, g_pre, w_up, w_conv, b_conv, w_down, g_post):
    bsz, seq, _ = x.shape
    tt = min(FFN_TILE, seq)
    n_t = seq // tt
    per = tt // CONV_HALO
    n_halo = seq // CONV_HALO
    return pl.pallas_call(
        _ffn_kernel,
        grid=(bsz, n_t),
        in_specs=[
            pl.BlockSpec((1, tt, D_MODEL), lambda b, t: (b, t, 0)),
            pl.BlockSpec((1, CONV_HALO, D_MODEL), lambda b, t: (b, jnp.maximum(t * per - 1, 0), 0)),
            pl.BlockSpec((1, CONV_HALO, D_MODEL), lambda b, t: (b, jnp.minimum((t + 1) * per, n_halo - 1), 0)),
            pl.BlockSpec((1, N_MOD, D_MODEL), lambda b, t: (b, 0, 0)),
            _const_spec((1, D_MODEL)),
            _const_spec((D_MODEL, 2 * D_FF)),
            _const_spec((3, 2 * D_FF)),
            _const_spec((1, 2 * D_FF)),
            _const_spec((D_FF, D_MODEL)),
            _const_spec((1, D_MODEL)),
        ],
        out_specs=pl.BlockSpec((1, tt, D_MODEL), lambda b, t: (b, t, 0)),
        out_shape=jax.ShapeDtypeStruct((bsz, seq, D_MODEL), F32),
        scratch_shapes=[pltpu.VMEM((tt, D_FF), BF16)],
        compiler_params=pltpu.CompilerParams(
            dimension_semantics=("arbitrary", "arbitrary"), vmem_limit_bytes=VMEM_LIMIT),
        name="ffn",
    )(x, x, x, mod, g_pre, w_up, w_conv, b_conv, w_down, g_post)


def _layer_params(l, g_pre_mix, g_post_mix, g_pre_ffn, g_post_ffn, w_in, w_s, b_s, g_vn, g_out_a, w_gf, b_gf, w_gb,
                  b_gb, g_out_b, w_out, w_up, w_conv, b_conv, w_down):
    row = lambda a: a[l].reshape(1, -1)
    w_lr = jnp.pad(w_in[l][:, COL_LR:COL_END], ((0, 0), (0, LR_PAD - (COL_END - COL_LR)))).astype(BF16)
    w_g = jnp.zeros((LR_PAD, 2 * GLA_K_WIDTH), F32)
    w_g = w_g.at[:GLA_GATE_RANK, :GLA_K_WIDTH].set(w_gf[l])
    w_g = w_g.at[GLA_GATE_RANK:2 * GLA_GATE_RANK, GLA_K_WIDTH:].set(w_gb[l]).astype(BF16)
    b_g = jnp.concatenate([b_gf[l], b_gb[l]]).reshape(1, -1)
    bs_full = jnp.broadcast_to(b_s[l][:, :, None], (SG_HEADS, SG_CHUNK, SG_HEAD_DIM))
    return dict(
        g_pre_mix=row(g_pre_mix), g_post_mix=row(g_post_mix), g_pre_ffn=row(g_pre_ffn), g_post_ffn=row(g_post_ffn),
        w_main=w_in[l][:, :COL_LR].astype(BF16), w_lr=w_lr, w_g=w_g, b_g=b_g,
        w_s=w_s[l].astype(BF16), bs_full=bs_full, g_vn=row(g_vn), g_out_a=row(g_out_a), g_out_b=row(g_out_b),
        w_out=w_out[l].astype(BF16), w_up=w_up[l].astype(BF16), w_conv=w_conv[l], b_conv=row(b_conv),
        w_down=w_down[l].astype(BF16))


def _layer(x, mod, p):
    packed = _mix_fwd(x, mod, p["g_pre_mix"], p["w_main"], p["w_lr"], p["w_g"], p["b_g"], p["w_s"], p["bs_full"],
                      p["g_vn"], p["g_out_a"])
    x = _mix_bwd(x, mod, packed, p["w_g"], p["b_g"], p["g_out_b"], p["w_out"], p["g_post_mix"])
    return _ffn(x, mod, p["g_pre_ffn"], p["w_up"], p["w_conv"], p["b_conv"], p["w_down"], p["g_post_ffn"])


def _trunks(xs, cs, w_ada, b_ada, *weights):
    depth = w_ada.shape[0]
    sizes = [c.shape[0] for c in cs]
    mod_all = _modulation(jnp.concatenate(cs, axis=0), w_ada, b_ada)
    xs = list(xs)
    for l in range(depth):
        p = _layer_params(l, *weights)
        start = 0
        for i, n in enumerate(sizes):
            mod = mod_all[l, start:start + n].reshape(n, N_MOD, D_MODEL)
            xs[i] = _layer(xs[i], mod, p)
            start += n
    return tuple(xs)


def kernel(x_prompt, x_sample, c_prompt, c_sample, w_ada, b_ada, g_pre_mix, g_post_mix, g_pre_ffn, g_post_ffn, w_in, w_s, b_s, g_vn, g_out_a, w_gf, b_gf, w_gb, b_gb, g_out_b, w_out, w_up, w_conv, b_conv, w_down):
    return _trunks((x_prompt, x_sample), (c_prompt, c_sample), w_ada, b_ada, g_pre_mix, g_post_mix, g_pre_ffn,
                   g_post_ffn, w_in, w_s, b_s, g_vn, g_out_a, w_gf, b_gf, w_gb, b_gb, g_out_b, w_out, w_up, w_conv,
                   b_conv, w_down)
```
